```python
import jax, jax.numpy as jnp
from jax import lax
import numpy as np

D_MODEL = 1024
BATCH = 16
SEQ = 2048
DEPTH = 4
DEC_BATCH = 32
DEC_SEQ = 2048
PAST_LEN = 128

HEAD_DIM = 64
A_HEADS = 8
A_KV_HEADS = 2
A_RADIUS = 128
A_BLOCK = 128
B_GROUPS = ((128, 1), (512, 4), (2048, 16))
B_HPG = 4
B_HEADS = B_HPG * len(B_GROUPS)
C_HEADS = 4
C_HEAD_DIM = 128
N_MEM = 256
ROPE_THETA = 500000.0
ROPE_DIMS = HEAD_DIM // 4
FFN_HIDDEN = ((8 * D_MODEL + 3 * 256 - 1) // (3 * 256)) * 256
N_BRANCH = 3
A_Q_W = A_HEADS * HEAD_DIM
A_KV_W = A_KV_HEADS * HEAD_DIM
B_W = B_HEADS * HEAD_DIM
C_Q_W = C_HEADS * C_HEAD_DIM
GATE_W = N_BRANCH * D_MODEL
IN_W = A_Q_W + 2 * A_KV_W + 3 * B_W + C_Q_W + GATE_W
EPS = 1e-6
NEG_INF = -1e30

kernel_name = "hybrid_gated_window_dilated_memory_encoder"


def _rmsnorm(x, g):
    xf = x.astype(jnp.float32)
    y = xf * lax.rsqrt(jnp.mean(xf * xf, axis=-1, keepdims=True) + EPS)
    return (y * g.astype(jnp.float32)).astype(x.dtype)


def _rope_tables(seq):
    inv_freq = ROPE_THETA ** (-jnp.arange(0, ROPE_DIMS, 2, dtype=jnp.float32) / ROPE_DIMS)
    ang = jnp.arange(seq, dtype=jnp.float32)[:, None] * inv_freq[None, :]
    return jnp.cos(ang), jnp.sin(ang)


def _rope(x, cos, sin):
    half = ROPE_DIMS // 2
    xf = x.astype(jnp.float32)
    x1, x2 = xf[..., :half], xf[..., half:ROPE_DIMS]
    c, s = cos[None, :, None, :], sin[None, :, None, :]
    out = jnp.concatenate([x1 * c - x2 * s, x2 * c + x1 * s, xf[..., ROPE_DIMS:]], axis=-1)
    return out.astype(x.dtype)


def _banded_attention(q, k, v, radius, block, sink=None):
    b, L, g, r, dh = q.shape
    nb = -(-L // block)
    lp = nb * block
    pad = lp - L
    q = jnp.pad(q, ((0, 0), (0, pad), (0, 0), (0, 0), (0, 0)))

    def kv_blocks(t):
        t = jnp.pad(t, ((0, 0), (block, block + pad), (0, 0), (0, 0)))
        t = t.reshape(b, nb + 2, block, g, dh)
        return jnp.concatenate([t[:, :-2], t[:, 1:-1], t[:, 2:]], axis=2)

    kb, vb = kv_blocks(k), kv_blocks(v)
    qb = q.reshape(b, nb, block, g, r, dh)
    s = jnp.einsum('bnqgrd,bnkgd->bngrqk', qb, kb).astype(jnp.float32) * (dh ** -0.5)
    n_idx = jnp.arange(nb)[:, None, None]
    q_pos = n_idx * block + jnp.arange(block)[None, :, None]
    k_pos = (n_idx - 1) * block + jnp.arange(3 * block)[None, None, :]
    mask = (jnp.abs(q_pos - k_pos) <= radius) & (k_pos >= 0) & (k_pos < L)
    s = jnp.where(mask[None, :, None, None], s, NEG_INF)
    m = jnp.max(s, axis=-1)
    if sink is not None:
        sk = sink.astype(jnp.float32)[None, None, :, :, None]
        m = jnp.maximum(m, sk)
    p = jnp.exp(s - m[..., None])
    denom = jnp.sum(p, axis=-1)
    if sink is not None:
        denom = denom + jnp.exp(sk - m)
    o = jnp.einsum('bngrqk,bnkgd->bnqgrd', p.astype(v.dtype), vb).astype(jnp.float32)
    o = o / jnp.moveaxis(denom, -1, 2)[..., None]
    o = o.reshape(b, lp, g, r, dh)[:, :L].astype(q.dtype)
    lse = jnp.moveaxis(m + jnp.log(denom), -1, 2).reshape(b, lp, g, r)[:, :L]
    return o, lse


def _windowed_gqa(q, k, v, sink):
    b, S = q.shape[0], q.shape[1]
    rep = A_HEADS // A_KV_HEADS
    qg = q.reshape(b, S, A_KV_HEADS, rep, HEAD_DIM)
    o, _ = _banded_attention(qg, k, v, A_RADIUS, A_BLOCK, sink.reshape(A_KV_HEADS, rep))
    return o.reshape(b, S, A_Q_W)


def _dilated_attention(q, k, v):
    b, S = q.shape[0], q.shape[1]
    outs, lses = [], []
    for gi, (window, dil) in enumerate(B_GROUPS):
        lo, hi = gi * B_HPG, (gi + 1) * B_HPG
        L = S // dil
        radius = window // (2 * dil)

        def gather(t):
            return (t[:, :, lo:hi].reshape(b, L, dil, B_HPG, HEAD_DIM)
                    .transpose(0, 2, 1, 3, 4).reshape(b * dil, L, B_HPG, HEAD_DIM))

        o, lse = _banded_attention(gather(q)[:, :, :, None], gather(k), gather(v), radius, radius)
        o = o.reshape(b, dil, L, B_HPG, HEAD_DIM).transpose(0, 2, 1, 3, 4).reshape(b, S, B_HPG, HEAD_DIM)
        lse = lse.reshape(b, dil, L, B_HPG).transpose(0, 2, 1, 3).reshape(b, S, B_HPG)
        outs.append(o)
        lses.append(lse)
    alpha = jax.nn.softmax(jnp.stack(lses, axis=0), axis=0)
    o = jnp.sum(alpha[..., None] * jnp.stack(outs, axis=0).astype(jnp.float32), axis=0)
    return o.reshape(b, S, B_HPG * HEAD_DIM).astype(q.dtype)


def _memory_attention(q, mem_n, w_mem_kv):
    b, S = q.shape[0], q.shape[1]
    kv = mem_n @ w_mem_kv
    mk, mv = jnp.split(kv, 2, axis=-1)
    mk = mk.reshape(b, N_MEM, C_HEADS, C_HEAD_DIM)
    mv = mv.reshape(b, N_MEM, C_HEADS, C_HEAD_DIM)
    s = jnp.einsum('bshd,bmhd->bhsm', q, mk).astype(jnp.float32) * (C_HEAD_DIM ** -0.5)
    p = jax.nn.softmax(s, axis=-1)
    o = jnp.einsum('bhsm,bmhd->bshd', p.astype(mv.dtype), mv)
    return o.reshape(b, S, C_Q_W)


def _layer(x, mem, cos, sin, g_mix_pre, g_mix_post, g_mem, w_in, sink_a, w_mem_kv,
           w_o_a, w_o_b, w_o_c, w_out, g_ffn_pre, g_ffn_post, w_ffn_in, w_ffn_out):
    b, S, _ = x.shape
    h = _rmsnorm(x, g_mix_pre)
    proj = h @ w_in
    cuts = [A_Q_W, A_KV_W, A_KV_W, B_W, B_W, B_W, C_Q_W]
    idx, off = [], 0
    for c in cuts:
        off += c
        idx.append(off)
    qa, ka, va, qb, kb, vb, qc, gates = jnp.split(proj, idx, axis=-1)
    qa = _rope(qa.reshape(b, S, A_HEADS, HEAD_DIM), cos, sin)
    ka = _rope(ka.reshape(b, S, A_KV_HEADS, HEAD_DIM), cos, sin)
    va = va.reshape(b, S, A_KV_HEADS, HEAD_DIM)
    qb = _rope(qb.reshape(b, S, B_HEADS, HEAD_DIM), cos, sin)
    kb = _rope(kb.reshape(b, S, B_HEADS, HEAD_DIM), cos, sin)
    vb = vb.reshape(b, S, B_HEADS, HEAD_DIM)
    qc = qc.reshape(b, S, C_HEADS, C_HEAD_DIM)

    o_a = _windowed_gqa(qa, ka, va, sink_a) @ w_o_a
    o_b = _dilated_attention(qb, kb, vb) @ w_o_b
    o_c = _memory_attention(qc, _rmsnorm(mem, g_mem), w_mem_kv) @ w_o_c

    gate = jax.nn.sigmoid(gates.reshape(b, S, N_BRANCH, D_MODEL))
    merged = gate[:, :, 0] * o_a + gate[:, :, 1] * o_b + gate[:, :, 2] * o_c
    x = x + _rmsnorm(merged @ w_out, g_mix_post)
    h = _rmsnorm(x, g_ffn_pre)
    gu = h @ w_ffn_in
    g_, u_ = jnp.split(gu, 2, axis=-1)
    y = (jax.nn.silu(g_) * u_) @ w_ffn_out
    return x + _rmsnorm(y, g_ffn_post)


def _trunk(x, mem, weights):
    cos, sin = _rope_tables(x.shape[1])
    for l in range(DEPTH):
        x = _layer(x, mem, cos, sin, *[w[l] for w in weights])
    return x


def setup_inputs(seed: int = 0) -> dict:
    key = jax.random.key(seed)
    ks = jax.random.split(key, 20)
    f32 = jnp.float32

    def dense(k, fan_in, fan_out):
        return jax.random.normal(k, (DEPTH, fan_in, fan_out), f32) * (fan_in ** -0.5)

    def gain(k):
        return 1.0 + 0.05 * jax.random.normal(k, (DEPTH, D_MODEL), f32)

    return {
        "x_prompt": jax.random.normal(ks[0], (BATCH, SEQ, D_MODEL), f32),
        "x_sample": jax.random.normal(ks[1], (DEC_BATCH, DEC_SEQ, D_MODEL), f32),
        "mem_prompt": jax.random.normal(ks[2], (BATCH, N_MEM, D_MODEL), f32),
        "mem_sample": jax.random.normal(ks[3], (DEC_BATCH, N_MEM, D_MODEL), f32),
        "norm_mix_pre": gain(ks[4]),
        "norm_mix_post": gain(ks[5]),
        "norm_mem": gain(ks[6]),
        "w_in": dense(ks[7], D_MODEL, IN_W),
        "sink_a": 0.5 * jax.random.normal(ks[8], (DEPTH, A_HEADS), f32),
        "w_mem_kv": dense(ks[9], D_MODEL, 2 * C_Q_W),
        "w_o_a": dense(ks[10], A_Q_W, D_MODEL),
        "w_o_b": dense(ks[11], B_HPG * HEAD_DIM, D_MODEL),
        "w_o_c": dense(ks[12], C_Q_W, D_MODEL),
        "w_out": dense(ks[13], D_MODEL, D_MODEL),
        "norm_ffn_pre": gain(ks[14]),
        "norm_ffn_post": gain(ks[15]),
        "w_ffn_in": dense(ks[16], D_MODEL, 2 * FFN_HIDDEN),
        "w_ffn_out": dense(ks[17], FFN_HIDDEN, D_MODEL),
    }


def reference(x_prompt, x_sample, mem_prompt, mem_sample, norm_mix_pre, norm_mix_post, norm_mem,
              w_in, sink_a, w_mem_kv, w_o_a, w_o_b, w_o_c, w_out, norm_ffn_pre, norm_ffn_post,
              w_ffn_in, w_ffn_out):
    weights = (norm_mix_pre, norm_mix_post, norm_mem, w_in, sink_a, w_mem_kv, w_o_a, w_o_b,
               w_o_c, w_out, norm_ffn_pre, norm_ffn_post, w_ffn_in, w_ffn_out)
    y_prompt = _trunk(x_prompt, mem_prompt, weights)
    y_sample = _trunk(x_sample, mem_sample, weights)
    return (y_prompt, y_sample)
```

```python
import functools

import jax
import jax.numpy as jnp
from jax import lax
from jax.experimental import pallas as pl
from jax.experimental.pallas import tpu as pltpu

D_MODEL = 1024
DEPTH = 4
HEAD_DIM = 64
A_HEADS = 8
A_KV_HEADS = 2
A_RADIUS = 128
B_GROUPS = ((128, 1), (512, 4), (2048, 16))
B_HPG = 4
C_HEADS = 4
C_HEAD_DIM = 128
N_MEM = 256
ROPE_THETA = 500000.0
ROPE_DIMS = HEAD_DIM // 4
ROPE_HALF = ROPE_DIMS // 2
FFN_HIDDEN = 2816
A_Q_W = A_HEADS * HEAD_DIM
A_KV_W = A_KV_HEADS * HEAD_DIM
B_GW = B_HPG * HEAD_DIM
B_W = B_GW * len(B_GROUPS)
C_Q_W = C_HEADS * C_HEAD_DIM
QKV_W = A_Q_W + 2 * A_KV_W + 3 * B_W + C_Q_W
EPS = 1e-6
NEG_INF = -1e30

LANES = 128
VMEM_LIMIT = 56 * 1024 * 1024

BF16 = jnp.bfloat16
F32 = jnp.float32

A_HEAD_ORDER = (0, 4, 1, 5, 2, 6, 3, 7)


def _const_spec(shape):
    nd = len(shape)
    return pl.BlockSpec(shape, lambda *_: (0,) * nd, pipeline_mode=pl.Buffered(1))


def _params(n_grid):
    return pltpu.CompilerParams(dimension_semantics=("arbitrary",) * n_grid,
                                vmem_limit_bytes=VMEM_LIMIT)


def _rmsnorm(x, g):
    return x * lax.rsqrt(jnp.mean(x * x, axis=-1, keepdims=True) + EPS) * g


def _dot(a, b):
    return jnp.dot(a, b, preferred_element_type=F32)


def _dot_nt(a, b):
    return lax.dot_general(a, b, (((1,), (1,)), ((), ())), preferred_element_type=F32)


def _memkv_kernel(mem_ref, g_ref, w_ref, o_ref):
    h = _rmsnorm(mem_ref[...], g_ref[...]).astype(BF16)
    o_ref[...] = _dot(h, w_ref[...]).astype(BF16)


def _memkv_call(mem, g_mem, w_mem_kv):
    b = mem.shape[0]
    return pl.pallas_call(
        _memkv_kernel,
        grid=(DEPTH, b),
        in_specs=[
            pl.BlockSpec((None, N_MEM, D_MODEL), lambda l, i: (i, 0, 0)),
            pl.BlockSpec((None, 1, D_MODEL), lambda l, i: (l, 0, 0)),
            pl.BlockSpec((None, D_MODEL, 2 * C_Q_W), lambda l, i: (l, 0, 0)),
        ],
        out_specs=pl.BlockSpec((None, None, N_MEM, 2 * C_Q_W), lambda l, i: (l, i, 0, 0)),
        out_shape=jax.ShapeDtypeStruct((DEPTH, b, N_MEM, 2 * C_Q_W), BF16),
        compiler_params=_params(2),
        name="memkv",
    )(mem, g_mem, w_mem_kv)


def _rope_tile(y, c, s_up, s_dn):
    return (y * c + pltpu.roll(y, LANES - ROPE_HALF, 1) * s_up + pltpu.roll(y, ROPE_HALF, 1) * s_dn)


def _proj_kernel(x_ref, g_ref, w_ref, rope_ref, mkv_ref,
                 qa_ref, ka_ref, va_ref,
                 qb0_ref, qb1_ref, qb2_ref, kb0_ref, kb1_ref, kb2_ref, vb0_ref, vb1_ref, vb2_ref,
                 oc_ref):
    h = _rmsnorm(x_ref[...], g_ref[...]).astype(BF16)

    def project(col0, width, out_ref, rope_base):
        y = _dot(h, w_ref[:, col0:col0 + width])
        for t in range(width // LANES):
            y_t = y[:, t * LANES:(t + 1) * LANES]
            if rope_base is not None:
                y_t = _rope_tile(y_t, rope_ref[rope_base], rope_ref[rope_base + 1], rope_ref[rope_base + 2])
            out_ref[:, t * LANES:(t + 1) * LANES] = y_t.astype(BF16)

    rope_q, rope_k = 0, 3
    col = 0
    project(col, A_Q_W, qa_ref, rope_q); col += A_Q_W
    project(col, A_KV_W, ka_ref, rope_k); col += A_KV_W
    project(col, A_KV_W, va_ref, None); col += A_KV_W
    for out_ref in (qb0_ref, qb1_ref, qb2_ref):
        project(col, B_GW, out_ref, rope_q); col += B_GW
    for out_ref in (kb0_ref, kb1_ref, kb2_ref):
        project(col, B_GW, out_ref, rope_k); col += B_GW
    for out_ref in (vb0_ref, vb1_ref, vb2_ref):
        project(col, B_GW, out_ref, None); col += B_GW

    qc = (_dot(h, w_ref[:, col:col + C_Q_W]) * (C_HEAD_DIM ** -0.5)).astype(BF16)
    for hd in range(C_HEADS):
        lo, hi = hd * C_HEAD_DIM, (hd + 1) * C_HEAD_DIM
        s = _dot_nt(qc[:, lo:hi], mkv_ref[:, lo:hi])
        m = jnp.max(s, axis=-1, keepdims=True)
        p = jnp.exp(s - m)
        denom = jnp.sum(p, axis=-1, keepdims=True)
        o = _dot(p.astype(BF16), mkv_ref[:, C_Q_W + lo:C_Q_W + hi])
        oc_ref[:, lo:hi] = (o * (1.0 / denom)).astype(BF16)


def _proj_call(x, g_pre, w_qkv, rope_tabs, mkv, layer, tm):
    b, s, _ = x.shape
    tok = lambda w: pl.BlockSpec((None, tm, w), lambda i, j: (i, j, 0))
    out_widths = (A_Q_W, A_KV_W, A_KV_W) + (B_GW,) * 9 + (C_Q_W,)
    return pl.pallas_call(
        _proj_kernel,
        grid=(b, s // tm),
        in_specs=[
            tok(D_MODEL),
            _const_spec((1, D_MODEL)),
            _const_spec((D_MODEL, QKV_W)),
            pl.BlockSpec((6, tm, LANES), lambda i, j: (0, j, 0)),
            pl.BlockSpec((None, None, N_MEM, 2 * C_Q_W), lambda i, j: (layer, i, 0, 0)),
        ],
        out_specs=[tok(w) for w in out_widths],
        out_shape=[jax.ShapeDtypeStruct((b, s, w), BF16) for w in out_widths],
        compiler_params=_params(2),
        name="proj",
    )(x, g_pre, w_qkv, rope_tabs, mkv)


def _head_lane_masks():
    lane = lax.broadcasted_iota(jnp.int32, (1, LANES), 1)
    lo = (lane < HEAD_DIM).astype(BF16)
    return lo, (1 - lo).astype(BF16)


def _band_mask(tq, kw, q0_minus_k0, radius):
    d = (lax.broadcasted_iota(jnp.int32, (tq, kw), 0) + q0_minus_k0
         - lax.broadcasted_iota(jnp.int32, (tq, kw), 1))
    return jnp.abs(d) <= radius


def _attend(q_t, k_m, v_m, mask, sink):
    s = jnp.where(mask, _dot_nt(q_t, k_m), NEG_INF)
    m = jnp.max(s, axis=-1, keepdims=True)
    if sink is not None:
        m = jnp.maximum(m, sink)
    p = jnp.exp(s - m)
    denom = jnp.sum(p, axis=-1, keepdims=True)
    if sink is not None:
        denom = denom + jnp.exp(sink - m)
    o = _dot(p.astype(BF16), v_m) * (1.0 / denom)
    return o, m + jnp.log(denom)


def _attn_a_kernel(sink_ref, q_ref, k_ref, v_ref, o_ref, *, tq, kw, seq):
    i = pl.program_id(1)
    start = pl.multiple_of(jnp.clip(i * tq - A_RADIUS, 0, seq - kw), LANES)
    lo, hi = _head_lane_masks()
    k = k_ref[pl.ds(start, kw), :]
    v = v_ref[pl.ds(start, kw), :]
    k_halves = (k * lo, k * hi)
    v_halves = (v * lo, v * hi)
    mask = _band_mask(tq, kw, i * tq - start, A_RADIUS)
    for t in range(A_Q_W // LANES):
        q_t = q_ref[:, t * LANES:(t + 1) * LANES]
        o = None
        for half in range(2):
            head = A_HEAD_ORDER[2 * t + half]
            o_h, _ = _attend(q_t, k_halves[half], v_halves[half], mask, sink_ref[head])
            o = o_h if o is None else o + o_h
        o_ref[:, t * LANES:(t + 1) * LANES] = o.astype(BF16)


def _attn_a_call(sink, qa, ka, va, tq):
    b, s, _ = qa.shape
    kw = tq + 2 * A_RADIUS
    return pl.pallas_call(
        functools.partial(_attn_a_kernel, tq=tq, kw=kw, seq=s),
        grid=(b, s // tq),
        in_specs=[
            pl.BlockSpec(memory_space=pltpu.SMEM),
            pl.BlockSpec((None, tq, A_Q_W), lambda i, j: (i, j, 0)),
            pl.BlockSpec((None, s, A_KV_W), lambda i, j: (i, 0, 0)),
            pl.BlockSpec((None, s, A_KV_W), lambda i, j: (i, 0, 0)),
        ],
        out_specs=pl.BlockSpec((None, tq, A_Q_W), lambda i, j: (i, j, 0)),
        out_shape=jax.ShapeDtypeStruct((b, s, A_Q_W), BF16),
        compiler_params=_params(2),
        name="attn_a",
    )(sink, qa, ka, va)


def _attn_b_kernel(q_ref, k_ref, v_ref, o_ref, lse_ref, *, tq, kw, length, radius):
    i = pl.program_id(2)
    start = pl.multiple_of(jnp.clip(i * tq - radius, 0, length - kw), radius)
    lo, hi = _head_lane_masks()
    lane_is_lo = lax.broadcasted_iota(jnp.int32, (1, LANES), 1) < HEAD_DIM
    mask = _band_mask(tq, kw, i * tq - start, radius)
    for t in range(B_GW // LANES):
        q_t = q_ref[:, t * LANES:(t + 1) * LANES]
        k = k_ref[pl.ds(start, kw), t * LANES:(t + 1) * LANES]
        v = v_ref[pl.ds(start, kw), t * LANES:(t + 1) * LANES]
        o_lo, lse_lo = _attend(q_t, k * lo, v * lo, mask, None)
        o_hi, lse_hi = _attend(q_t, k * hi, v * hi, mask, None)
        o_ref[:, t * LANES:(t + 1) * LANES] = (o_lo + o_hi).astype(BF16)
        lse_ref[:, t * LANES:(t + 1) * LANES] = jnp.where(lane_is_lo, lse_lo, lse_hi)


def _attn_b_call(q, k, v, window, dil, tq):
    b, s, _ = q.shape
    length = s // dil
    radius = window // (2 * dil)
    tq = min(tq, length)
    kw = min(tq + 2 * radius, length)
    view = lambda a: a.reshape(b, length, dil * B_GW)
    q_spec = pl.BlockSpec((None, tq, B_GW), lambda i, c, j: (i, j, c))
    kv_spec = pl.BlockSpec((None, length, B_GW), lambda i, c, j: (i, 0, c))
    o, lse = pl.pallas_call(
        functools.partial(_attn_b_kernel, tq=tq, kw=kw, length=length, radius=radius),
        grid=(b, dil, length // tq),
        in_specs=[q_spec, kv_spec, kv_spec],
        out_specs=[q_spec, q_spec],
        out_shape=[jax.ShapeDtypeStruct((b, length, dil * B_GW), BF16),
                   jax.ShapeDtypeStruct((b, length, dil * B_GW), F32)],
        compiler_params=_params(3),
        name=f"attn_b_dil{dil}",
    )(view(q), view(k), view(v))
    return o.reshape(b, s, B_GW), lse.reshape(b, s, B_GW)


def _merge_kernel(x_ref, oa_ref, ob0_ref, ob1_ref, ob2_ref, l0_ref, l1_ref, l2_ref, oc_ref,
                  g_pre_ref, g_post_ref, w_gate_ref, w_oa_ref, w_ob_ref, w_oc_ref, w_out_ref,
                  out_ref, merged_ref, *, chunk):
    x = x_ref[...]
    h = _rmsnorm(x, g_pre_ref[...]).astype(BF16)

    l0, l1, l2 = l0_ref[...], l1_ref[...], l2_ref[...]
    m = jnp.maximum(jnp.maximum(l0, l1), l2)
    e0, e1, e2 = jnp.exp(l0 - m), jnp.exp(l1 - m), jnp.exp(l2 - m)
    num = (e0 * ob0_ref[...].astype(F32) + e1 * ob1_ref[...].astype(F32) + e2 * ob2_ref[...].astype(F32))
    ob = (num * (1.0 / (e0 + e1 + e2))).astype(BF16)

    oa = oa_ref[...]
    oc = oc_ref[...]
    for c0 in range(0, D_MODEL, chunk):
        cols = slice(c0, c0 + chunk)
        acc = None
        for br, (o_br, w_ref) in enumerate(((oa, w_oa_ref), (ob, w_ob_ref), (oc, w_oc_ref))):
            gate = jax.nn.sigmoid(_dot(h, w_gate_ref[:, br * D_MODEL + c0:br * D_MODEL + c0 + chunk]))
            term = gate * _dot(o_br, w_ref[:, cols])
            acc = term if acc is None else acc + term
        merged_ref[:, cols] = acc.astype(BF16)

    z = _dot(merged_ref[...], w_out_ref[...])
    out_ref[...] = x + _rmsnorm(z, g_post_ref[...])


def _merge_call(x, oa, obs, lses, oc, g_pre, g_post, w_gate, w_oa, w_ob, w_oc, w_out, tm):
    b, s, _ = x.shape
    tok = lambda w: pl.BlockSpec((None, tm, w), lambda i, j: (i, j, 0))
    return pl.pallas_call(
        functools.partial(_merge_kernel, chunk=512),
        grid=(b, s // tm),
        in_specs=[tok(D_MODEL), tok(A_Q_W), tok(B_GW), tok(B_GW), tok(B_GW), tok(B_GW), tok(B_GW), tok(B_GW),
                  tok(C_Q_W),
                  _const_spec((1, D_MODEL)), _const_spec((1, D_MODEL)),
                  _const_spec((D_MODEL, 3 * D_MODEL)),
                  _const_spec((A_Q_W, D_MODEL)), _const_spec((B_GW, D_MODEL)), _const_spec((C_Q_W, D_MODEL)),
                  _const_spec((D_MODEL, D_MODEL))],
        out_specs=tok(D_MODEL),
        out_shape=jax.ShapeDtypeStruct((b, s, D_MODEL), F32),
        scratch_shapes=[pltpu.VMEM((tm, D_MODEL), BF16)],
        compiler_params=_params(2),
        name="merge",
    )(x, oa, *obs, *lses, oc, g_pre, g_post, w_gate, w_oa, w_ob, w_oc, w_out)


def _ffn_kernel(x_ref, g_pre_ref, g_post_ref, w_in_ref, w_out_ref, out_ref, act_ref, *, chunk):
    x = x_ref[...]
    h = _rmsnorm(x, g_pre_ref[...]).astype(BF16)
    for c0 in range(0, FFN_HIDDEN, chunk):
        g = _dot(h, w_in_ref[:, c0:c0 + chunk])
        u = _dot(h, w_in_ref[:, FFN_HIDDEN + c0:FFN_HIDDEN + c0 + chunk])
        act_ref[:, c0:c0 + chunk] = (g * jax.nn.sigmoid(g) * u).astype(BF16)
    y = _dot(act_ref[...], w_out_ref[...])
    out_ref[...] = x + _rmsnorm(y, g_post_ref[...])


def _ffn_call(x, g_pre, g_post, w_in, w_out, tm):
    b, s, _ = x.shape
    tok = pl.BlockSpec((None, tm, D_MODEL), lambda i, j: (i, j, 0))
    return pl.pallas_call(
        functools.partial(_ffn_kernel, chunk=256),
        grid=(b, s // tm),
        in_specs=[tok, _const_spec((1, D_MODEL)), _const_spec((1, D_MODEL)),
                  _const_spec((D_MODEL, 2 * FFN_HIDDEN)), _const_spec((FFN_HIDDEN, D_MODEL))],
        out_specs=tok,
        out_shape=jax.ShapeDtypeStruct((b, s, D_MODEL), F32),
        scratch_shapes=[pltpu.VMEM((tm, FFN_HIDDEN), BF16)],
        compiler_params=_params(2),
        name="ffn",
    )(x, g_pre, g_post, w_in, w_out)


def _rope_tables(seq):
    inv_freq = ROPE_THETA ** (-jnp.arange(0, ROPE_DIMS, 2, dtype=F32) / ROPE_DIMS)
    ang = jnp.arange(seq, dtype=F32)[:, None] * inv_freq[None, :]
    cos, sin = jnp.cos(ang), jnp.sin(ang)
    pad = jnp.zeros((seq, HEAD_DIM - ROPE_DIMS), F32)
    zero = jnp.zeros_like(sin)
    c = jnp.concatenate([cos, cos, pad + 1.0], axis=-1)
    s_up = jnp.concatenate([-sin, zero, pad], axis=-1)
    s_dn = jnp.concatenate([zero, sin, pad], axis=-1)
    tabs = jnp.stack([jnp.tile(t, (1, LANES // HEAD_DIM)) for t in (c, s_up, s_dn)])
    return jnp.concatenate([tabs * (HEAD_DIM ** -0.5), tabs], axis=0)


def _layer_weights(l, norm_mix_pre, norm_mix_post, w_in, sink_a, w_o_a, w_o_b, w_o_c, w_out,
                   norm_ffn_pre, norm_ffn_post, w_ffn_in, w_ffn_out):
    order = jnp.array(A_HEAD_ORDER)
    w_in_l = w_in[l]
    w_qa = w_in_l[:, :A_Q_W].reshape(D_MODEL, A_HEADS, HEAD_DIM)[:, order].reshape(D_MODEL, A_Q_W)
    w_qkv = jnp.concatenate([w_qa, w_in_l[:, A_Q_W:QKV_W]], axis=1).astype(BF16)
    w_oa = w_o_a[l].reshape(A_HEADS, HEAD_DIM, D_MODEL)[order].reshape(A_Q_W, D_MODEL).astype(BF16)
    row = lambda g: g[l].reshape(1, D_MODEL)
    return dict(
        g_mix_pre=row(norm_mix_pre), g_mix_post=row(norm_mix_post), w_qkv=w_qkv,
        w_gate=w_in_l[:, QKV_W:].astype(BF16), sink=sink_a[l],
        w_oa=w_oa, w_ob=w_o_b[l].astype(BF16), w_oc=w_o_c[l].astype(BF16), w_out=w_out[l].astype(BF16),
        g_ffn_pre=row(norm_ffn_pre), g_ffn_post=row(norm_ffn_post),
        w_ffn_in=w_ffn_in[l].astype(BF16), w_ffn_out=w_ffn_out[l].astype(BF16))


def _trunk(x, mem, layers, g_mem, w_mem_kv, rope_tabs):
    mkv = _memkv_call(mem, g_mem, w_mem_kv)
    for l, w in enumerate(layers):
        (qa, ka, va, qb0, qb1, qb2, kb0, kb1, kb2, vb0, vb1, vb2, oc) = _proj_call(
            x, w["g_mix_pre"], w["w_qkv"], rope_tabs, mkv, l, tm=512)
        oa = _attn_a_call(w["sink"], qa, ka, va, tq=128)
        obs, lses = [], []
        for (window, dil), q, k, v in zip(B_GROUPS, (qb0, qb1, qb2), (kb0, kb1, kb2), (vb0, vb1, vb2)):
            o, lse = _attn_b_call(q, k, v, window, dil, tq=128)
            obs.append(o)
            lses.append(lse)
        x = _merge_call(x, oa, obs, lses, oc, w["g_mix_pre"], w["g_mix_post"], w["w_gate"],
                        w["w_oa"], w["w_ob"], w["w_oc"], w["w_out"], tm=512)
        x = _ffn_call(x, w["g_ffn_pre"], w["g_ffn_post"], w["w_ffn_in"], w["w_ffn_out"], tm=512)
    return x


@jax.jit
def kernel(x_prompt, x_sample, mem_prompt, mem_sample, norm_mix_pre, norm_mix_post, norm_mem, w_in, sink_a,
           w_mem_kv, w_o_a, w_o_b, w_o_c, w_out, norm_ffn_pre, norm_ffn_post, w_ffn_in, w_ffn_out):
    layers = [_layer_weights(l, norm_mix_pre, norm_mix_post, w_in, sink_a, w_o_a, w_o_b, w_o_c, w_out,
                             norm_ffn_pre, norm_ffn_post, w_ffn_in, w_ffn_out) for l in range(DEPTH)]
    g_mem = norm_mem.reshape(DEPTH, 1, D_MODEL)
    w_mkv = w_mem_kv.astype(BF16)
    rope_tabs = _rope_tables(x_prompt.shape[1])
    y_prompt = _trunk(x_prompt, mem_prompt, layers, g_mem, w_mkv, rope_tabs)
    y_sample = _trunk(x_sample, mem_sample, layers, g_mem, w_mkv, rope_tabs)
    return (y_prompt, y_sample)
```

```python
import functools

import jax
import jax.numpy as jnp
from jax import lax
from jax.experimental import pallas as pl
from jax.experimental.pallas import tpu as pltpu

D_MODEL = 1024
DEPTH = 4
HEAD_DIM = 64
A_HEADS = 8
A_KV_HEADS = 2
A_RADIUS = 128
B_GROUPS = ((128, 1), (512, 4), (2048, 16))
B_HPG = 4
C_HEADS = 4
C_HEAD_DIM = 128
N_MEM = 256
ROPE_THETA = 500000.0
ROPE_DIMS = HEAD_DIM // 4
ROPE_HALF = ROPE_DIMS // 2
FFN_HIDDEN = 2816
A_Q_W = A_HEADS * HEAD_DIM
A_KV_W = A_KV_HEADS * HEAD_DIM
B_GW = B_HPG * HEAD_DIM
B_W = B_GW * len(B_GROUPS)
C_Q_W = C_HEADS * C_HEAD_DIM
QKV_W = A_Q_W + 2 * A_KV_W + 3 * B_W + C_Q_W
EPS = 1e-6
NEG_INF = -1e30

LANES = 128
VMEM_LIMIT = 56 * 1024 * 1024
A_SLABS = A_Q_W // LANES
B_SLABS = B_W // LANES
G_SLABS = B_GW // LANES
CLASS_STRIDE = 4

BF16 = jnp.bfloat16
F32 = jnp.float32

A_HEAD_ORDER = (0, 4, 1, 5, 2, 6, 3, 7)


def _const_spec(shape):
    nd = len(shape)
    return pl.BlockSpec(shape, lambda *_: (0,) * nd, pipeline_mode=pl.Buffered(1))


def _params(n_grid):
    return pltpu.CompilerParams(dimension_semantics=("arbitrary",) * n_grid,
                                vmem_limit_bytes=VMEM_LIMIT)


def _rmsnorm(x, g):
    return x * lax.rsqrt(jnp.mean(x * x, axis=-1, keepdims=True) + EPS) * g


def _dot(a, b):
    return jnp.dot(a, b, preferred_element_type=F32)


def _dot_nt(a, b):
    return lax.dot_general(a, b, (((1,), (1,)), ((), ())), preferred_element_type=F32)


def _memkv_kernel(mem_ref, g_ref, w_ref, o_ref):
    h = _rmsnorm(mem_ref[...], g_ref[...]).astype(BF16)
    o_ref[...] = _dot(h, w_ref[...]).astype(BF16)


def _memkv_call(mem, g_mem, w_mem_kv):
    b = mem.shape[0]
    return pl.pallas_call(
        _memkv_kernel,
        grid=(DEPTH, b),
        in_specs=[
            pl.BlockSpec((None, N_MEM, D_MODEL), lambda l, i: (i, 0, 0)),
            pl.BlockSpec((None, 1, D_MODEL), lambda l, i: (l, 0, 0)),
            pl.BlockSpec((None, D_MODEL, 2 * C_Q_W), lambda l, i: (l, 0, 0)),
        ],
        out_specs=pl.BlockSpec((None, None, N_MEM, 2 * C_Q_W), lambda l, i: (l, i, 0, 0)),
        out_shape=jax.ShapeDtypeStruct((DEPTH, b, N_MEM, 2 * C_Q_W), BF16),
        compiler_params=_params(2),
        name="memkv",
    )(mem, g_mem, w_mem_kv)


def _rope_tile(y, c, s_up, s_dn):
    return (y * c + pltpu.roll(y, LANES - ROPE_HALF, 1) * s_up + pltpu.roll(y, ROPE_HALF, 1) * s_dn)


def _proj_kernel(x_ref, g_ref, w_ref, rope_ref, mkv_ref,
                 qa_ref, ka_ref, va_ref, qb_ref, kb_ref, vb_ref, oc_ref):
    h = _rmsnorm(x_ref[...], g_ref[...]).astype(BF16)

    def project(col0, width, store, rope_base):
        y = _dot(h, w_ref[:, col0:col0 + width])
        for t in range(width // LANES):
            y_t = y[:, t * LANES:(t + 1) * LANES]
            if rope_base is not None:
                y_t = _rope_tile(y_t, rope_ref[rope_base], rope_ref[rope_base + 1], rope_ref[rope_base + 2])
            store(t, y_t.astype(BF16))

    def slab_store(ref):
        def store(t, val):
            ref[t] = val
        return store

    def flat_store(ref):
        def store(t, val):
            ref[:, t * LANES:(t + 1) * LANES] = val
        return store

    rope_q, rope_k = 0, 3
    col = 0
    project(col, A_Q_W, slab_store(qa_ref), rope_q); col += A_Q_W
    project(col, A_KV_W, flat_store(ka_ref), rope_k); col += A_KV_W
    project(col, A_KV_W, flat_store(va_ref), None); col += A_KV_W
    project(col, B_W, slab_store(qb_ref), rope_q); col += B_W
    project(col, B_W, slab_store(kb_ref), rope_k); col += B_W
    project(col, B_W, slab_store(vb_ref), None); col += B_W

    qc = (_dot(h, w_ref[:, col:col + C_Q_W]) * (C_HEAD_DIM ** -0.5)).astype(BF16)
    for hd in range(C_HEADS):
        lo, hi = hd * C_HEAD_DIM, (hd + 1) * C_HEAD_DIM
        s = _dot_nt(qc[:, lo:hi], mkv_ref[:, lo:hi])
        m = jnp.max(s, axis=-1, keepdims=True)
        p = jnp.exp(s - m)
        denom = jnp.sum(p, axis=-1, keepdims=True)
        o = _dot(p.astype(BF16), mkv_ref[:, C_Q_W + lo:C_Q_W + hi])
        oc_ref[:, lo:hi] = (o * (1.0 / denom)).astype(BF16)


def _proj_call(x, g_pre, w_qkv, rope_tabs, mkv, layer, tm):
    b, s, _ = x.shape
    tok = lambda w: pl.BlockSpec((None, tm, w), lambda i, j: (i, j, 0))
    slab = lambda n: pl.BlockSpec((None, n, tm, LANES), lambda i, j: (i, 0, j, 0))
    slab_shape = lambda n: jax.ShapeDtypeStruct((b, n, s, LANES), BF16)
    flat_shape = lambda w: jax.ShapeDtypeStruct((b, s, w), BF16)
    return pl.pallas_call(
        _proj_kernel,
        grid=(b, s // tm),
        in_specs=[
            tok(D_MODEL),
            _const_spec((1, D_MODEL)),
            _const_spec((D_MODEL, QKV_W)),
            pl.BlockSpec((6, tm, LANES), lambda i, j: (0, j, 0)),
            pl.BlockSpec((None, None, N_MEM, 2 * C_Q_W), lambda i, j: (layer, i, 0, 0)),
        ],
        out_specs=[slab(A_SLABS), tok(A_KV_W), tok(A_KV_W), slab(B_SLABS), slab(B_SLABS), slab(B_SLABS),
                   tok(C_Q_W)],
        out_shape=[slab_shape(A_SLABS), flat_shape(A_KV_W), flat_shape(A_KV_W),
                   slab_shape(B_SLABS), slab_shape(B_SLABS), slab_shape(B_SLABS), flat_shape(C_Q_W)],
        compiler_params=_params(2),
        name="proj",
    )(x, g_pre, w_qkv, rope_tabs, mkv)


def _head_lane_masks():
    lane = lax.broadcasted_iota(jnp.int32, (1, LANES), 1)
    lo = (lane < HEAD_DIM).astype(BF16)
    return lo, (1 - lo).astype(BF16)


def _band_bias(tq, kw, q0_minus_k0, radius):
    d = (lax.broadcasted_iota(jnp.int32, (tq, kw), 0) + q0_minus_k0
         - lax.broadcasted_iota(jnp.int32, (tq, kw), 1))
    return jnp.where(jnp.abs(d) <= radius, 0.0, NEG_INF).astype(F32)


def _attend(q_t, k_pair, v_pair, bias, lane_is_lo, sinks):
    kw = k_pair.shape[0] // 2
    s = _dot_nt(q_t, k_pair)
    ms, ps = [], []
    for half in range(2):
        s_h = s[:, half * kw:(half + 1) * kw] + bias
        m_h = jnp.broadcast_to(jnp.max(s_h, axis=-1, keepdims=True), (s_h.shape[0], LANES))
        if sinks is not None:
            m_h = jnp.maximum(m_h, sinks[half])
        ps.append(jnp.exp(s_h - jnp.concatenate([m_h] * (kw // LANES), axis=1)))
        ms.append(m_h)
    m = jnp.where(lane_is_lo, ms[0], ms[1])
    r = _dot(jnp.concatenate(ps, axis=1).astype(BF16), v_pair)
    denom = r[:, LANES:]
    if sinks is not None:
        denom = denom + jnp.exp(jnp.where(lane_is_lo, sinks[0], sinks[1]) - m)
    return r[:, :LANES] * (1.0 / denom), m, denom


def _pair_kv(k, v, lo, hi):
    k_pair = jnp.concatenate([k * lo, k * hi], axis=0)
    v_pair = jnp.concatenate([jnp.concatenate([v * lo, jnp.broadcast_to(lo, v.shape)], axis=1),
                              jnp.concatenate([v * hi, jnp.broadcast_to(hi, v.shape)], axis=1)], axis=0)
    return k_pair, v_pair


def _attn_a_kernel(sink_ref, q_ref, k_ref, v_ref, o_ref, *, tq, kw, seq, unroll):
    lo, hi = _head_lane_masks()
    lane_is_lo = lax.broadcasted_iota(jnp.int32, (1, LANES), 1) < HEAD_DIM
    sinks = [jnp.full((tq, LANES), sink_ref[h], F32) for h in range(A_HEADS)]

    def block(i, carry):
        q0 = pl.multiple_of(i * tq, tq)
        start = pl.multiple_of(jnp.clip(q0 - A_RADIUS, 0, seq - kw), LANES)
        bias = _band_bias(tq, kw, q0 - start, A_RADIUS)
        k_pair, v_pair = _pair_kv(k_ref[pl.ds(start, kw), :], v_ref[pl.ds(start, kw), :], lo, hi)
        for t in range(A_SLABS):
            q_t = q_ref[t, pl.ds(q0, tq), :]
            o, _, _ = _attend(q_t, k_pair, v_pair, bias, lane_is_lo,
                              (sinks[A_HEAD_ORDER[2 * t]], sinks[A_HEAD_ORDER[2 * t + 1]]))
            o_ref[t, pl.ds(q0, tq), :] = o.astype(BF16)
        return carry

    lax.fori_loop(0, seq // tq, block, 0, unroll=unroll)


def _attn_a_call(sink, qa, ka, va, tq, unroll):
    b, _, s, _ = qa.shape
    kw = tq + 2 * A_RADIUS
    slab = pl.BlockSpec((None, A_SLABS, s, LANES), lambda i: (i, 0, 0, 0))
    kv = pl.BlockSpec((None, s, A_KV_W), lambda i: (i, 0, 0))
    return pl.pallas_call(
        functools.partial(_attn_a_kernel, tq=tq, kw=kw, seq=s, unroll=unroll),
        grid=(b,),
        in_specs=[pl.BlockSpec(memory_space=pltpu.SMEM), slab, kv, kv],
        out_specs=slab,
        out_shape=jax.ShapeDtypeStruct((b, A_SLABS, s, LANES), BF16),
        compiler_params=_params(1),
        name="attn_a",
    )(sink, qa, ka, va)


def _attn_b_kernel(q_ref, k_ref, v_ref, o_ref, lse_ref, *scratch, dil, length, radius, tq, kw, unroll):
    lo, hi = _head_lane_masks()
    lane_is_lo = lax.broadcasted_iota(jnp.int32, (1, LANES), 1) < HEAD_DIM
    nblk = length // tq
    seq = length * dil
    per = seq // CLASS_STRIDE
    two_level = dil == CLASS_STRIDE * CLASS_STRIDE
    assert dil in (1, CLASS_STRIDE, CLASS_STRIDE * CLASS_STRIDE)

    def deinterleave(dst_ref, src_ref):
        for t in range(G_SLABS):
            for c0 in range(CLASS_STRIDE):
                dst_ref[t, c0 * per:(c0 + 1) * per, :] = src_ref[t, pl.ds(c0, per, stride=CLASS_STRIDE), :]

    def interleave(dst_ref, src_ref):
        for t in range(G_SLABS):
            for c0 in range(CLASS_STRIDE):
                dst_ref[t, pl.ds(c0, per, stride=CLASS_STRIDE), :] = src_ref[t, c0 * per:(c0 + 1) * per, :]

    if dil == CLASS_STRIDE:
        qs_ref, ks_ref, vs_ref, os_ref = scratch
        for dst_ref, src_ref in ((qs_ref, q_ref), (ks_ref, k_ref), (vs_ref, v_ref)):
            dst_ref[...] = src_ref[...].astype(F32)
    elif two_level:
        nat_ref, qs_ref, ks_ref, vs_ref, os_ref, ls_ref = scratch
        for dst_ref, src_ref in ((qs_ref, q_ref), (ks_ref, k_ref), (vs_ref, v_ref)):
            nat_ref[...] = src_ref[...].astype(F32)
            deinterleave(dst_ref, nat_ref)

    def unit(u, carry):
        c = u // nblk if nblk > 1 else u
        i = u - c * nblk if nblk > 1 else 0
        q0 = i * tq
        start = jnp.clip(q0 - radius, 0, length - kw)
        bias = _band_bias(tq, kw, q0 - start, radius)
        if dil > 1:
            base = (c % CLASS_STRIDE) * per + c // CLASS_STRIDE if two_level else c
            q_rows = pl.ds(base + q0 * CLASS_STRIDE, tq, stride=CLASS_STRIDE)
            k_rows = pl.ds(base + start * CLASS_STRIDE, kw, stride=CLASS_STRIDE)
        else:
            q_rows = pl.ds(pl.multiple_of(q0, tq), tq)
            k_rows = pl.ds(pl.multiple_of(start, radius), kw)
        for t in range(G_SLABS):
            if dil > 1:
                q_t = qs_ref[t, q_rows, :].astype(BF16)
                k = ks_ref[t, k_rows, :].astype(BF16)
                v = vs_ref[t, k_rows, :].astype(BF16)
            else:
                q_t, k, v = q_ref[t, q_rows, :], k_ref[t, k_rows, :], v_ref[t, k_rows, :]
            k_pair, v_pair = _pair_kv(k, v, lo, hi)
            o, m, denom = _attend(q_t, k_pair, v_pair, bias, lane_is_lo, None)
            if dil > 1:
                os_ref[t, q_rows, :] = o
            else:
                o_ref[t, q_rows, :] = o.astype(BF16)
            (ls_ref if two_level else lse_ref)[t, q_rows, :] = m + jnp.log(denom)
        return carry

    lax.fori_loop(0, dil * nblk, unit, 0, unroll=unroll)
    if dil == CLASS_STRIDE:
        o_ref[...] = os_ref[...].astype(BF16)
    elif two_level:
        interleave(nat_ref, os_ref)
        o_ref[...] = nat_ref[...].astype(BF16)
        interleave(lse_ref, ls_ref)


def _attn_b_call(q, k, v, group, tq, unroll):
    window, dil = B_GROUPS[group]
    b, _, s, _ = q.shape
    length = s // dil
    radius = window // (2 * dil)
    tq = min(tq, length)
    kw = min(tq + 2 * radius, length)
    in_slab = pl.BlockSpec((None, G_SLABS, s, LANES), lambda i: (i, group, 0, 0))
    out_slab = pl.BlockSpec((None, G_SLABS, s, LANES), lambda i: (i, 0, 0, 0))
    n_staging = {1: 0, CLASS_STRIDE: 4, CLASS_STRIDE * CLASS_STRIDE: 6}[dil]
    return pl.pallas_call(
        functools.partial(_attn_b_kernel, dil=dil, length=length, radius=radius, tq=tq, kw=kw, unroll=unroll),
        grid=(b,),
        in_specs=[in_slab, in_slab, in_slab],
        out_specs=[out_slab, out_slab],
        out_shape=[jax.ShapeDtypeStruct((b, G_SLABS, s, LANES), BF16),
                   jax.ShapeDtypeStruct((b, G_SLABS, s, LANES), F32)],
        scratch_shapes=[pltpu.VMEM((G_SLABS, s, LANES), F32)] * n_staging,
        compiler_params=_params(1),
        name=f"attn_b_dil{dil}",
    )(q, k, v)


def _lanes(ref):
    return jnp.concatenate([ref[t] for t in range(ref.shape[0])], axis=-1)


def _merge_kernel(x_ref, oa_ref, ob0_ref, ob1_ref, ob2_ref, l0_ref, l1_ref, l2_ref, oc_ref,
                  g_pre_ref, g_post_ref, w_gate_ref, w_oa_ref, w_ob_ref, w_oc_ref, w_out_ref,
                  out_ref, merged_ref, *, chunk):
    x = x_ref[...]
    h = _rmsnorm(x, g_pre_ref[...]).astype(BF16)

    l0, l1, l2 = _lanes(l0_ref), _lanes(l1_ref), _lanes(l2_ref)
    m = jnp.maximum(jnp.maximum(l0, l1), l2)
    e0, e1, e2 = jnp.exp(l0 - m), jnp.exp(l1 - m), jnp.exp(l2 - m)
    num = (e0 * _lanes(ob0_ref).astype(F32) + e1 * _lanes(ob1_ref).astype(F32)
           + e2 * _lanes(ob2_ref).astype(F32))
    ob = (num * (1.0 / (e0 + e1 + e2))).astype(BF16)

    oa = _lanes(oa_ref)
    oc = oc_ref[...]
    for c0 in range(0, D_MODEL, chunk):
        cols = slice(c0, c0 + chunk)
        acc = None
        for br, (o_br, w_ref) in enumerate(((oa, w_oa_ref), (ob, w_ob_ref), (oc, w_oc_ref))):
            gate = jax.nn.sigmoid(_dot(h, w_gate_ref[:, br * D_MODEL + c0:br * D_MODEL + c0 + chunk]))
            term = gate * _dot(o_br, w_ref[:, cols])
            acc = term if acc is None else acc + term
        merged_ref[:, cols] = acc.astype(BF16)

    z = _dot(merged_ref[...], w_out_ref[...])
    out_ref[...] = x + _rmsnorm(z, g_post_ref[...])


def _merge_call(x, oa, obs, lses, oc, g_pre, g_post, w_gate, w_oa, w_ob, w_oc, w_out, tm):
    b, s, _ = x.shape
    tok = lambda w: pl.BlockSpec((None, tm, w), lambda i, j: (i, j, 0))
    slab = lambda n: pl.BlockSpec((None, n, tm, LANES), lambda i, j: (i, 0, j, 0))
    g = slab(G_SLABS)
    return pl.pallas_call(
        functools.partial(_merge_kernel, chunk=512),
        grid=(b, s // tm),
        in_specs=[tok(D_MODEL), slab(A_SLABS), g, g, g, g, g, g, tok(C_Q_W),
                  _const_spec((1, D_MODEL)), _const_spec((1, D_MODEL)),
                  _const_spec((D_MODEL, 3 * D_MODEL)),
                  _const_spec((A_Q_W, D_MODEL)), _const_spec((B_GW, D_MODEL)), _const_spec((C_Q_W, D_MODEL)),
                  _const_spec((D_MODEL, D_MODEL))],
        out_specs=tok(D_MODEL),
        out_shape=jax.ShapeDtypeStruct((b, s, D_MODEL), F32),
        scratch_shapes=[pltpu.VMEM((tm, D_MODEL), BF16)],
        compiler_params=_params(2),
        name="merge",
    )(x, oa, *obs, *lses, oc, g_pre, g_post, w_gate, w_oa, w_ob, w_oc, w_out)


def _ffn_kernel(x_ref, g_pre_ref, g_post_ref, w_in_ref, w_out_ref, out_ref, act_ref, *, chunk):
    x = x_ref[...]
    h = _rmsnorm(x, g_pre_ref[...]).astype(BF16)
    for c0 in range(0, FFN_HIDDEN, chunk):
        g = _dot(h, w_in_ref[:, c0:c0 + chunk])
        u = _dot(h, w_in_ref[:, FFN_HIDDEN + c0:FFN_HIDDEN + c0 + chunk])
        act_ref[:, c0:c0 + chunk] = (g * jax.nn.sigmoid(g) * u).astype(BF16)
    y = _dot(act_ref[...], w_out_ref[...])
    out_ref[...] = x + _rmsnorm(y, g_post_ref[...])


def _ffn_call(x, g_pre, g_post, w_in, w_out, tm):
    b, s, _ = x.shape
    tok = pl.BlockSpec((None, tm, D_MODEL), lambda i, j: (i, j, 0))
    return pl.pallas_call(
        functools.partial(_ffn_kernel, chunk=256),
        grid=(b, s // tm),
        in_specs=[tok, _const_spec((1, D_MODEL)), _const_spec((1, D_MODEL)),
                  _const_spec((D_MODEL, 2 * FFN_HIDDEN)), _const_spec((FFN_HIDDEN, D_MODEL))],
        out_specs=tok,
        out_shape=jax.ShapeDtypeStruct((b, s, D_MODEL), F32),
        scratch_shapes=[pltpu.VMEM((tm, FFN_HIDDEN), BF16)],
        compiler_params=_params(2),
        name="ffn",
    )(x, g_pre, g_post, w_in, w_out)


def _rope_tables(seq):
    inv_freq = ROPE_THETA ** (-jnp.arange(0, ROPE_DIMS, 2, dtype=F32) / ROPE_DIMS)
    ang = jnp.arange(seq, dtype=F32)[:, None] * inv_freq[None, :]
    cos, sin = jnp.cos(ang), jnp.sin(ang)
    pad = jnp.zeros((seq, HEAD_DIM - ROPE_DIMS), F32)
    zero = jnp.zeros_like(sin)
    c = jnp.concatenate([cos, cos, pad + 1.0], axis=-1)
    s_up = jnp.concatenate([-sin, zero, pad], axis=-1)
    s_dn = jnp.concatenate([zero, sin, pad], axis=-1)
    tabs = jnp.stack([jnp.tile(t, (1, LANES // HEAD_DIM)) for t in (c, s_up, s_dn)])
    return jnp.concatenate([tabs * (HEAD_DIM ** -0.5), tabs], axis=0)


def _layer_weights(l, norm_mix_pre, norm_mix_post, w_in, sink_a, w_o_a, w_o_b, w_o_c, w_out,
                   norm_ffn_pre, norm_ffn_post, w_ffn_in, w_ffn_out):
    order = jnp.array(A_HEAD_ORDER)
    w_in_l = w_in[l]
    w_qa = w_in_l[:, :A_Q_W].reshape(D_MODEL, A_HEADS, HEAD_DIM)[:, order].reshape(D_MODEL, A_Q_W)
    w_qkv = jnp.concatenate([w_qa, w_in_l[:, A_Q_W:QKV_W]], axis=1).astype(BF16)
    w_oa = w_o_a[l].reshape(A_HEADS, HEAD_DIM, D_MODEL)[order].reshape(A_Q_W, D_MODEL).astype(BF16)
    row = lambda g: g[l].reshape(1, D_MODEL)
    return dict(
        g_mix_pre=row(norm_mix_pre), g_mix_post=row(norm_mix_post), w_qkv=w_qkv,
        w_gate=w_in_l[:, QKV_W:].astype(BF16), sink=sink_a[l],
        w_oa=w_oa, w_ob=w_o_b[l].astype(BF16), w_oc=w_o_c[l].astype(BF16), w_out=w_out[l].astype(BF16),
        g_ffn_pre=row(norm_ffn_pre), g_ffn_post=row(norm_ffn_post),
        w_ffn_in=w_ffn_in[l].astype(BF16), w_ffn_out=w_ffn_out[l].astype(BF16))


def _trunk(x, mem, layers, g_mem, w_mem_kv, rope_tabs):
    mkv = _memkv_call(mem, g_mem, w_mem_kv)
    for l, w in enumerate(layers):
        qa, ka, va, qb, kb, vb, oc = _proj_call(x, w["g_mix_pre"], w["w_qkv"], rope_tabs, mkv, l, tm=512)
        oa = _attn_a_call(w["sink"], qa, ka, va, tq=128, unroll=4)
        obs, lses = [], []
        for group in range(len(B_GROUPS)):
            o, lse = _attn_b_call(qb, kb, vb, group, tq=128, unroll=4)
            obs.append(o)
            lses.append(lse)
        x = _merge_call(x, oa, obs, lses, oc, w["g_mix_pre"], w["g_mix_post"], w["w_gate"],
                        w["w_oa"], w["w_ob"], w["w_oc"], w["w_out"], tm=512)
        x = _ffn_call(x, w["g_ffn_pre"], w["g_ffn_post"], w["w_ffn_in"], w["w_ffn_out"], tm=512)
    return x


@jax.jit
def kernel(x_prompt, x_sample, mem_prompt, mem_sample, norm_mix_pre, norm_mix_post, norm_mem, w_in, sink_a,
           w_mem_kv, w_o_a, w_o_b, w_o_c, w_out, norm_ffn_pre, norm_ffn_post, w_ffn_in, w_ffn_out):
    layers = [_layer_weights(l, norm_mix_pre, norm_mix_post, w_in, sink_a, w_o_a, w_o_b, w_o_c, w_out,
                             norm_ffn_pre, norm_ffn_post, w_ffn_in, w_ffn_out) for l in range(DEPTH)]
    g_mem = norm_mem.reshape(DEPTH, 1, D_MODEL)
    w_mkv = w_mem_kv.astype(BF16)
    rope_tabs = _rope_tables(x_prompt.shape[1])
    y_prompt = _trunk(x_prompt, mem_prompt, layers, g_mem, w_mkv, rope_tabs)
    y_sample = _trunk(x_sample, mem_sample, layers, g_mem, w_mkv, rope_tabs)
    return (y_prompt, y_sample)
```

```python
import functools
import math

import jax
import jax.numpy as jnp
from jax import lax
from jax.experimental import pallas as pl
from jax.experimental.pallas import tpu as pltpu

D_MODEL = 1024
DEPTH = 4
HEAD_DIM = 64
A_HEADS = 8
A_KV_HEADS = 2
A_RADIUS = 128
B_GROUPS = ((128, 1), (512, 4), (2048, 16))
B_HPG = 4
C_HEADS = 4
C_HEAD_DIM = 128
N_MEM = 256
ROPE_THETA = 500000.0
ROPE_DIMS = HEAD_DIM // 4
ROPE_HALF = ROPE_DIMS // 2
FFN_HIDDEN = 2816
A_Q_W = A_HEADS * HEAD_DIM
A_KV_W = A_KV_HEADS * HEAD_DIM
B_GW = B_HPG * HEAD_DIM
B_W = B_GW * len(B_GROUPS)
C_Q_W = C_HEADS * C_HEAD_DIM
QKV_W = A_Q_W + 2 * A_KV_W + 3 * B_W + C_Q_W
EPS = 1e-6
NEG_INF = -1e30
LOG2E = math.log2(math.e)

LANES = 128
VMEM_LIMIT = 56 * 1024 * 1024
A_SLABS = A_Q_W // LANES
B_SLABS = B_W // LANES
G_SLABS = B_GW // LANES
CLASS_STRIDE = 4

TOKEN_TILE = 1024
PROJ_SUBTILES = 2
MERGE_SUBTILES = 4
FFN_SUBTILES = 4
FFN_CHUNK = 256
ATTN_TQ = 128
ATTN_UNROLL = 4

BF16 = jnp.bfloat16
F32 = jnp.float32

A_HEAD_ORDER = (0, 4, 1, 5, 2, 6, 3, 7)


def _const_spec(shape):
    nd = len(shape)
    return pl.BlockSpec(shape, lambda *_: (0,) * nd, pipeline_mode=pl.Buffered(1))


def _params(n_grid):
    return pltpu.CompilerParams(dimension_semantics=("arbitrary",) * n_grid,
                                vmem_limit_bytes=VMEM_LIMIT)


def _rmsnorm(x, g):
    return x * lax.rsqrt(jnp.mean(x * x, axis=-1, keepdims=True) + EPS) * g


def _dot(a, b):
    return jnp.dot(a, b, preferred_element_type=F32)


def _dot_nt(a, b):
    return lax.dot_general(a, b, (((1,), (1,)), ((), ())), preferred_element_type=F32)


def _x_specs(xs, tm):
    if len(xs) == 1:
        return [pl.BlockSpec((None, tm, D_MODEL), lambda i, j: (i, j, 0))]
    b0 = xs[0].shape[0]
    last_j = xs[0].shape[1] // tm - 1
    first = pl.BlockSpec((None, tm, D_MODEL),
                         lambda i, j: (jnp.minimum(i, b0 - 1), jnp.where(i < b0, j, last_j), 0))
    second = pl.BlockSpec((None, tm, D_MODEL),
                          lambda i, j: (jnp.maximum(i - b0, 0), jnp.where(i < b0, 0, j), 0))
    return [first, second]


def _read_x(x_refs, split, rows):
    if len(x_refs) == 1:
        return x_refs[0][rows, :]
    return jnp.where(pl.program_id(0) < split, x_refs[0][rows, :], x_refs[1][rows, :])


def _memkv_kernel(mem_ref, g_ref, w_ref, o_ref):
    h = _rmsnorm(mem_ref[...], g_ref[...]).astype(BF16)
    o_ref[...] = _dot(h, w_ref[...]).astype(BF16)


def _memkv_call(mem, g_mem, w_mem_kv):
    b = mem.shape[0]
    return pl.pallas_call(
        _memkv_kernel,
        grid=(DEPTH, b),
        in_specs=[
            pl.BlockSpec((None, N_MEM, D_MODEL), lambda l, i: (i, 0, 0)),
            pl.BlockSpec((None, 1, D_MODEL), lambda l, i: (l, 0, 0)),
            pl.BlockSpec((None, D_MODEL, 2 * C_Q_W), lambda l, i: (l, 0, 0)),
        ],
        out_specs=pl.BlockSpec((None, None, N_MEM, 2 * C_Q_W), lambda l, i: (l, i, 0, 0)),
        out_shape=jax.ShapeDtypeStruct((DEPTH, b, N_MEM, 2 * C_Q_W), BF16),
        compiler_params=_params(2),
        name="memkv",
    )(mem, g_mem, w_mem_kv)


def _rope_tile(y, c, s_up, s_dn):
    return (y * c + pltpu.roll(y, LANES - ROPE_HALF, 1) * s_up + pltpu.roll(y, ROPE_HALF, 1) * s_dn)


def _proj_kernel(*refs, n_x, split, n_sub):
    x_refs = refs[:n_x]
    (g_ref, w_ref, rope_ref, mkv_ref, qa_ref, ka_ref, va_ref, qb_ref, kb_ref, vb_ref, oc_ref) = refs[n_x:]
    sub_rows = x_refs[0].shape[0] // n_sub
    for sub in range(n_sub):
        rows = slice(sub * sub_rows, (sub + 1) * sub_rows)
        h = _rmsnorm(_read_x(x_refs, split, rows), g_ref[...]).astype(BF16)

        def project(col0, width, store, rope_base):
            y = _dot(h, w_ref[:, col0:col0 + width])
            for t in range(width // LANES):
                y_t = y[:, t * LANES:(t + 1) * LANES]
                if rope_base is not None:
                    y_t = _rope_tile(y_t, rope_ref[rope_base, rows, :], rope_ref[rope_base + 1, rows, :],
                                     rope_ref[rope_base + 2, rows, :])
                store(t, y_t.astype(BF16))

        def slab_store(ref):
            def store(t, val):
                ref[t, rows, :] = val
            return store

        def flat_store(ref):
            def store(t, val):
                ref[rows, t * LANES:(t + 1) * LANES] = val
            return store

        rope_q, rope_k = 0, 3
        col = 0
        project(col, A_Q_W, slab_store(qa_ref), rope_q); col += A_Q_W
        project(col, A_KV_W, flat_store(ka_ref), rope_k); col += A_KV_W
        project(col, A_KV_W, flat_store(va_ref), None); col += A_KV_W
        project(col, B_W, slab_store(qb_ref), rope_q); col += B_W
        project(col, B_W, slab_store(kb_ref), rope_k); col += B_W
        project(col, B_W, slab_store(vb_ref), None); col += B_W

        qc = (_dot(h, w_ref[:, col:col + C_Q_W]) * (C_HEAD_DIM ** -0.5 * LOG2E)).astype(BF16)
        for hd in range(C_HEADS):
            lo, hi = hd * C_HEAD_DIM, (hd + 1) * C_HEAD_DIM
            s = _dot_nt(qc[:, lo:hi], mkv_ref[:, lo:hi])
            m = jnp.max(s, axis=-1, keepdims=True)
            p = jnp.exp2(s - m)
            denom = jnp.sum(p, axis=-1, keepdims=True)
            o = _dot(p.astype(BF16), mkv_ref[:, C_Q_W + lo:C_Q_W + hi])
            oc_ref[rows, lo:hi] = (o * (1.0 / denom)).astype(BF16)


def _proj_call(xs, g_pre, w_qkv, rope_tabs, mkv, layer):
    b = sum(x.shape[0] for x in xs)
    s = xs[0].shape[1]
    tm = TOKEN_TILE
    tok = lambda w: pl.BlockSpec((None, tm, w), lambda i, j: (i, j, 0))
    slab = lambda n: pl.BlockSpec((None, n, tm, LANES), lambda i, j: (i, 0, j, 0))
    slab_shape = lambda n: jax.ShapeDtypeStruct((b, n, s, LANES), BF16)
    flat_shape = lambda w: jax.ShapeDtypeStruct((b, s, w), BF16)
    return pl.pallas_call(
        functools.partial(_proj_kernel, n_x=len(xs), split=xs[0].shape[0], n_sub=PROJ_SUBTILES),
        grid=(b, s // tm),
        in_specs=_x_specs(xs, tm) + [
            _const_spec((1, D_MODEL)),
            _const_spec((D_MODEL, QKV_W)),
            pl.BlockSpec((6, tm, LANES), lambda i, j: (0, j, 0)),
            pl.BlockSpec((None, None, N_MEM, 2 * C_Q_W), lambda i, j: (layer, i, 0, 0)),
        ],
        out_specs=[slab(A_SLABS), tok(A_KV_W), tok(A_KV_W), slab(B_SLABS), slab(B_SLABS), slab(B_SLABS),
                   tok(C_Q_W)],
        out_shape=[slab_shape(A_SLABS), flat_shape(A_KV_W), flat_shape(A_KV_W),
                   slab_shape(B_SLABS), slab_shape(B_SLABS), slab_shape(B_SLABS), flat_shape(C_Q_W)],
        compiler_params=_params(2),
        name="proj",
    )(*xs, g_pre, w_qkv, rope_tabs, mkv)


def _head_lane_masks():
    lane = lax.broadcasted_iota(jnp.int32, (1, LANES), 1)
    lo = (lane < HEAD_DIM).astype(BF16)
    return lo, (1 - lo).astype(BF16)


def _band_bias(tq, kw, q0_minus_k0, radius):
    d = (lax.broadcasted_iota(jnp.int32, (tq, kw), 0) + q0_minus_k0
         - lax.broadcasted_iota(jnp.int32, (tq, kw), 1))
    return jnp.where(jnp.abs(d) <= radius, 0.0, NEG_INF).astype(F32)


def _attend(q_t, k_pair, v_pair, bias, lane_is_lo, sinks):
    kw = k_pair.shape[0] // 2
    s = _dot_nt(q_t, k_pair)
    ms, ps = [], []
    for half in range(2):
        s_h = (s[:, half * kw:(half + 1) * kw] + bias).astype(BF16)
        m_h = jnp.broadcast_to(jnp.max(s_h, axis=-1, keepdims=True), (s_h.shape[0], LANES))
        if sinks is not None:
            m_h = jnp.maximum(m_h, sinks[half].astype(BF16))
        ps.append(jnp.exp2(s_h - jnp.concatenate([m_h] * (kw // LANES), axis=1)))
        ms.append(m_h.astype(F32))
    m = jnp.where(lane_is_lo, ms[0], ms[1])
    r = _dot(jnp.concatenate(ps, axis=1), v_pair)
    denom = r[:, LANES:]
    if sinks is not None:
        denom = denom + jnp.exp2(jnp.where(lane_is_lo, sinks[0], sinks[1]) - m)
    return r[:, :LANES] * (1.0 / denom), m, denom


def _pair_kv(k, v, lo, hi):
    k_pair = jnp.concatenate([k * lo, k * hi], axis=0)
    v_pair = jnp.concatenate([jnp.concatenate([v * lo, jnp.broadcast_to(lo, v.shape)], axis=1),
                              jnp.concatenate([v * hi, jnp.broadcast_to(hi, v.shape)], axis=1)], axis=0)
    return k_pair, v_pair


def _attn_a_kernel(sink_ref, q_ref, k_ref, v_ref, o_ref, *, tq, kw, seq):
    lo, hi = _head_lane_masks()
    lane_is_lo = lax.broadcasted_iota(jnp.int32, (1, LANES), 1) < HEAD_DIM
    sinks = [jnp.full((tq, LANES), sink_ref[h] * LOG2E, F32) for h in range(A_HEADS)]

    def block(i, carry):
        q0 = pl.multiple_of(i * tq, tq)
        start = pl.multiple_of(jnp.clip(q0 - A_RADIUS, 0, seq - kw), LANES)
        bias = _band_bias(tq, kw, q0 - start, A_RADIUS)
        k_pair, v_pair = _pair_kv(k_ref[pl.ds(start, kw), :], v_ref[pl.ds(start, kw), :], lo, hi)
        for t in range(A_SLABS):
            q_t = q_ref[t, pl.ds(q0, tq), :]
            o, _, _ = _attend(q_t, k_pair, v_pair, bias, lane_is_lo,
                              (sinks[A_HEAD_ORDER[2 * t]], sinks[A_HEAD_ORDER[2 * t + 1]]))
            o_ref[t, pl.ds(q0, tq), :] = o.astype(BF16)
        return carry

    lax.fori_loop(0, seq // tq, block, 0, unroll=ATTN_UNROLL)


def _attn_a_call(sink, qa, ka, va):
    b, _, s, _ = qa.shape
    tq = ATTN_TQ
    kw = tq + 2 * A_RADIUS
    slab = pl.BlockSpec((None, A_SLABS, s, LANES), lambda i: (i, 0, 0, 0))
    kv = pl.BlockSpec((None, s, A_KV_W), lambda i: (i, 0, 0))
    return pl.pallas_call(
        functools.partial(_attn_a_kernel, tq=tq, kw=kw, seq=s),
        grid=(b,),
        in_specs=[pl.BlockSpec(memory_space=pltpu.SMEM), slab, kv, kv],
        out_specs=slab,
        out_shape=jax.ShapeDtypeStruct((b, A_SLABS, s, LANES), BF16),
        compiler_params=_params(1),
        name="attn_a",
    )(sink, qa, ka, va)


def _attn_b_kernel(q_ref, k_ref, v_ref, o_ref, lse_ref, *scratch, dil, length, radius, tq, kw):
    lo, hi = _head_lane_masks()
    lane_is_lo = lax.broadcasted_iota(jnp.int32, (1, LANES), 1) < HEAD_DIM
    nblk = length // tq
    seq = length * dil
    per = seq // CLASS_STRIDE
    two_level = dil == CLASS_STRIDE * CLASS_STRIDE
    assert dil in (1, CLASS_STRIDE, CLASS_STRIDE * CLASS_STRIDE)

    def deinterleave(dst_ref, src_ref):
        for t in range(G_SLABS):
            for c0 in range(CLASS_STRIDE):
                dst_ref[t, c0 * per:(c0 + 1) * per, :] = src_ref[t, pl.ds(c0, per, stride=CLASS_STRIDE), :]

    def interleave(dst_ref, src_ref):
        for t in range(G_SLABS):
            for c0 in range(CLASS_STRIDE):
                dst_ref[t, pl.ds(c0, per, stride=CLASS_STRIDE), :] = src_ref[t, c0 * per:(c0 + 1) * per, :]

    if dil == CLASS_STRIDE:
        qs_ref, ks_ref, vs_ref, os_ref = scratch
        for dst_ref, src_ref in ((qs_ref, q_ref), (ks_ref, k_ref), (vs_ref, v_ref)):
            dst_ref[...] = src_ref[...].astype(F32)
    elif two_level:
        nat_ref, qs_ref, ks_ref, vs_ref, os_ref, ls_ref = scratch
        for dst_ref, src_ref in ((qs_ref, q_ref), (ks_ref, k_ref), (vs_ref, v_ref)):
            nat_ref[...] = src_ref[...].astype(F32)
            deinterleave(dst_ref, nat_ref)

    def unit(u, carry):
        c = u // nblk if nblk > 1 else u
        i = u - c * nblk if nblk > 1 else 0
        q0 = i * tq
        start = jnp.clip(q0 - radius, 0, length - kw)
        bias = _band_bias(tq, kw, q0 - start, radius)
        if dil > 1:
            base = (c % CLASS_STRIDE) * per + c // CLASS_STRIDE if two_level else c
            q_rows = pl.ds(base + q0 * CLASS_STRIDE, tq, stride=CLASS_STRIDE)
            k_rows = pl.ds(base + start * CLASS_STRIDE, kw, stride=CLASS_STRIDE)
        else:
            q_rows = pl.ds(pl.multiple_of(q0, tq), tq)
            k_rows = pl.ds(pl.multiple_of(start, radius), kw)
        for t in range(G_SLABS):
            if dil > 1:
                q_t = qs_ref[t, q_rows, :].astype(BF16)
                k = ks_ref[t, k_rows, :].astype(BF16)
                v = vs_ref[t, k_rows, :].astype(BF16)
            else:
                q_t, k, v = q_ref[t, q_rows, :], k_ref[t, k_rows, :], v_ref[t, k_rows, :]
            k_pair, v_pair = _pair_kv(k, v, lo, hi)
            o, m, denom = _attend(q_t, k_pair, v_pair, bias, lane_is_lo, None)
            if dil > 1:
                os_ref[t, q_rows, :] = o
            else:
                o_ref[t, q_rows, :] = o.astype(BF16)
            (ls_ref if two_level else lse_ref)[t, q_rows, :] = m + jnp.log2(denom)
        return carry

    lax.fori_loop(0, dil * nblk, unit, 0, unroll=ATTN_UNROLL)
    if dil == CLASS_STRIDE:
        o_ref[...] = os_ref[...].astype(BF16)
    elif two_level:
        interleave(nat_ref, os_ref)
        o_ref[...] = nat_ref[...].astype(BF16)
        interleave(lse_ref, ls_ref)


def _attn_b_call(q, k, v, group):
    window, dil = B_GROUPS[group]
    b, _, s, _ = q.shape
    length = s // dil
    radius = window // (2 * dil)
    tq = min(ATTN_TQ, length)
    kw = min(tq + 2 * radius, length)
    in_slab = pl.BlockSpec((None, G_SLABS, s, LANES), lambda i: (i, group, 0, 0))
    out_slab = pl.BlockSpec((None, G_SLABS, s, LANES), lambda i: (i, 0, 0, 0))
    n_staging = {1: 0, CLASS_STRIDE: 4, CLASS_STRIDE * CLASS_STRIDE: 6}[dil]
    return pl.pallas_call(
        functools.partial(_attn_b_kernel, dil=dil, length=length, radius=radius, tq=tq, kw=kw),
        grid=(b,),
        in_specs=[in_slab, in_slab, in_slab],
        out_specs=[out_slab, out_slab],
        out_shape=[jax.ShapeDtypeStruct((b, G_SLABS, s, LANES), BF16),
                   jax.ShapeDtypeStruct((b, G_SLABS, s, LANES), F32)],
        scratch_shapes=[pltpu.VMEM((G_SLABS, s, LANES), F32)] * n_staging,
        compiler_params=_params(1),
        name=f"attn_b_dil{dil}",
    )(q, k, v)


def _lanes(ref, rows):
    return jnp.concatenate([ref[t, rows, :] for t in range(ref.shape[0])], axis=-1)


def _merge_kernel(*refs, n_x, split, n_sub):
    x_refs = refs[:n_x]
    (oa_ref, ob0_ref, ob1_ref, ob2_ref, l0_ref, l1_ref, l2_ref, oc_ref,
     g_pre_ref, g_post_ref, w_gate_ref, w_oa_ref, w_ob_ref, w_oc_ref, w_out_ref, out_ref) = refs[n_x:]
    sub_rows = out_ref.shape[0] // n_sub
    for sub in range(n_sub):
        rows = slice(sub * sub_rows, (sub + 1) * sub_rows)
        x = _read_x(x_refs, split, rows)
        h = _rmsnorm(x, g_pre_ref[...]).astype(BF16)

        l0, l1, l2 = _lanes(l0_ref, rows), _lanes(l1_ref, rows), _lanes(l2_ref, rows)
        m = jnp.maximum(jnp.maximum(l0, l1), l2)
        e0, e1, e2 = jnp.exp2(l0 - m), jnp.exp2(l1 - m), jnp.exp2(l2 - m)
        num = (e0 * _lanes(ob0_ref, rows).astype(F32) + e1 * _lanes(ob1_ref, rows).astype(F32)
               + e2 * _lanes(ob2_ref, rows).astype(F32))
        ob = (num * (1.0 / (e0 + e1 + e2))).astype(BF16)

        merged = None
        for br, (o_br, w_ref) in enumerate(((_lanes(oa_ref, rows), w_oa_ref), (ob, w_ob_ref),
                                            (oc_ref[rows, :], w_oc_ref))):
            gate = jax.nn.sigmoid(_dot(h, w_gate_ref[:, br * D_MODEL:(br + 1) * D_MODEL]))
            term = gate * _dot(o_br, w_ref[...])
            merged = term if merged is None else merged + term
        z = _dot(merged.astype(BF16), w_out_ref[...])
        out_ref[rows, :] = x + _rmsnorm(z, g_post_ref[...])


def _merge_call(xs, oa, obs, lses, oc, g_pre, g_post, w_gate, w_oa, w_ob, w_oc, w_out):
    b, _, s, _ = oa.shape
    tm = TOKEN_TILE
    tok = lambda w: pl.BlockSpec((None, tm, w), lambda i, j: (i, j, 0))
    slab = lambda n: pl.BlockSpec((None, n, tm, LANES), lambda i, j: (i, 0, j, 0))
    g = slab(G_SLABS)
    return pl.pallas_call(
        functools.partial(_merge_kernel, n_x=len(xs), split=xs[0].shape[0], n_sub=MERGE_SUBTILES),
        grid=(b, s // tm),
        in_specs=_x_specs(xs, tm) + [
            slab(A_SLABS), g, g, g, g, g, g, tok(C_Q_W),
            _const_spec((1, D_MODEL)), _const_spec((1, D_MODEL)),
            _const_spec((D_MODEL, 3 * D_MODEL)),
            _const_spec((A_Q_W, D_MODEL)), _const_spec((B_GW, D_MODEL)), _const_spec((C_Q_W, D_MODEL)),
            _const_spec((D_MODEL, D_MODEL))],
        out_specs=tok(D_MODEL),
        out_shape=jax.ShapeDtypeStruct((b, s, D_MODEL), F32),
        compiler_params=_params(2),
        name="merge",
    )(*xs, oa, *obs, *lses, oc, g_pre, g_post, w_gate, w_oa, w_ob, w_oc, w_out)


def _ffn_kernel(x_ref, g_pre_ref, g_post_ref, w_in_ref, w_out_ref, out_ref, act_ref, *, n_sub):
    sub_rows = x_ref.shape[0] // n_sub
    for sub in range(n_sub):
        rows = slice(sub * sub_rows, (sub + 1) * sub_rows)
        x = x_ref[rows, :]
        h = _rmsnorm(x, g_pre_ref[...]).astype(BF16)
        for c0 in range(0, FFN_HIDDEN, FFN_CHUNK):
            g = _dot(h, w_in_ref[:, c0:c0 + FFN_CHUNK])
            u = _dot(h, w_in_ref[:, FFN_HIDDEN + c0:FFN_HIDDEN + c0 + FFN_CHUNK])
            act_ref[rows, c0:c0 + FFN_CHUNK] = (g * jax.nn.sigmoid(g) * u).astype(BF16)
        y = _dot(act_ref[rows, :], w_out_ref[...])
        out_ref[rows, :] = x + _rmsnorm(y, g_post_ref[...])


def _ffn_call(x, g_pre, g_post, w_in, w_out, batch_offset, batch):
    s = x.shape[1]
    tm = TOKEN_TILE
    return pl.pallas_call(
        functools.partial(_ffn_kernel, n_sub=FFN_SUBTILES),
        grid=(batch, s // tm),
        in_specs=[pl.BlockSpec((None, tm, D_MODEL), lambda i, j: (i + batch_offset, j, 0)),
                  _const_spec((1, D_MODEL)), _const_spec((1, D_MODEL)),
                  _const_spec((D_MODEL, 2 * FFN_HIDDEN)), _const_spec((FFN_HIDDEN, D_MODEL))],
        out_specs=pl.BlockSpec((None, tm, D_MODEL), lambda i, j: (i, j, 0)),
        out_shape=jax.ShapeDtypeStruct((batch, s, D_MODEL), F32),
        scratch_shapes=[pltpu.VMEM((tm, FFN_HIDDEN), BF16)],
        compiler_params=_params(2),
        name="ffn",
    )(x, g_pre, g_post, w_in, w_out)


def _rope_tables(seq):
    inv_freq = ROPE_THETA ** (-jnp.arange(0, ROPE_DIMS, 2, dtype=F32) / ROPE_DIMS)
    ang = jnp.arange(seq, dtype=F32)[:, None] * inv_freq[None, :]
    cos, sin = jnp.cos(ang), jnp.sin(ang)
    pad = jnp.zeros((seq, HEAD_DIM - ROPE_DIMS), F32)
    zero = jnp.zeros_like(sin)
    c = jnp.concatenate([cos, cos, pad + 1.0], axis=-1)
    s_up = jnp.concatenate([-sin, zero, pad], axis=-1)
    s_dn = jnp.concatenate([zero, sin, pad], axis=-1)
    tabs = jnp.stack([jnp.tile(t, (1, LANES // HEAD_DIM)) for t in (c, s_up, s_dn)])
    return jnp.concatenate([tabs * (HEAD_DIM ** -0.5 * LOG2E), tabs], axis=0)


def _layer_weights(l, norm_mix_pre, norm_mix_post, w_in, sink_a, w_o_a, w_o_b, w_o_c, w_out,
                   norm_ffn_pre, norm_ffn_post, w_ffn_in, w_ffn_out):
    order = jnp.array(A_HEAD_ORDER)
    w_in_l = w_in[l]
    w_qa = w_in_l[:, :A_Q_W].reshape(D_MODEL, A_HEADS, HEAD_DIM)[:, order].reshape(D_MODEL, A_Q_W)
    w_qkv = jnp.concatenate([w_qa, w_in_l[:, A_Q_W:QKV_W]], axis=1).astype(BF16)
    w_oa = w_o_a[l].reshape(A_HEADS, HEAD_DIM, D_MODEL)[order].reshape(A_Q_W, D_MODEL).astype(BF16)
    row = lambda g: g[l].reshape(1, D_MODEL)
    return dict(
        g_mix_pre=row(norm_mix_pre), g_mix_post=row(norm_mix_post), w_qkv=w_qkv,
        w_gate=w_in_l[:, QKV_W:].astype(BF16), sink=sink_a[l],
        w_oa=w_oa, w_ob=w_o_b[l].astype(BF16), w_oc=w_o_c[l].astype(BF16), w_out=w_out[l].astype(BF16),
        g_ffn_pre=row(norm_ffn_pre), g_ffn_post=row(norm_ffn_post),
        w_ffn_in=w_ffn_in[l].astype(BF16), w_ffn_out=w_ffn_out[l].astype(BF16))


@jax.jit
def kernel(x_prompt, x_sample, mem_prompt, mem_sample, norm_mix_pre, norm_mix_post, norm_mem, w_in, sink_a,
           w_mem_kv, w_o_a, w_o_b, w_o_c, w_out, norm_ffn_pre, norm_ffn_post, w_ffn_in, w_ffn_out):
    layers = [_layer_weights(l, norm_mix_pre, norm_mix_post, w_in, sink_a, w_o_a, w_o_b, w_o_c, w_out,
                             norm_ffn_pre, norm_ffn_post, w_ffn_in, w_ffn_out) for l in range(DEPTH)]
    rope_tabs = _rope_tables(x_prompt.shape[1])
    mkv = _memkv_call(jnp.concatenate([mem_prompt, mem_sample], axis=0),
                      norm_mem.reshape(DEPTH, 1, D_MODEL), w_mem_kv.astype(BF16))
    batches = (x_prompt.shape[0], x_sample.shape[0])
    xs = (x_prompt, x_sample)
    for l, w in enumerate(layers):
        qa, ka, va, qb, kb, vb, oc = _proj_call(xs, w["g_mix_pre"], w["w_qkv"], rope_tabs, mkv, l)
        oa = _attn_a_call(w["sink"], qa, ka, va)
        obs, lses = zip(*[_attn_b_call(qb, kb, vb, group) for group in range(len(B_GROUPS))])
        x = _merge_call(xs, oa, obs, lses, oc, w["g_mix_pre"], w["g_mix_post"], w["w_gate"],
                        w["w_oa"], w["w_ob"], w["w_oc"], w["w_out"])
        ffn = functools.partial(_ffn_call, x, w["g_ffn_pre"], w["g_ffn_post"], w["w_ffn_in"], w["w_ffn_out"])
        if l + 1 < DEPTH:
            xs = (ffn(0, sum(batches)),)
    return (ffn(0, batches[0]), ffn(batches[0], batches[1]))
```

```python
import functools
import math

import jax
import jax.numpy as jnp
from jax import lax
from jax.experimental import pallas as pl
from jax.experimental.pallas import tpu as pltpu

D_MODEL = 1024
DEPTH = 4
HEAD_DIM = 64
A_HEADS = 8
A_KV_HEADS = 2
A_RADIUS = 128
B_GROUPS = ((128, 1), (512, 4), (2048, 16))
B_HPG = 4
C_HEADS = 4
C_HEAD_DIM = 128
N_MEM = 256
ROPE_THETA = 500000.0
ROPE_DIMS = HEAD_DIM // 4
ROPE_HALF = ROPE_DIMS // 2
FFN_HIDDEN = 2816
A_Q_W = A_HEADS * HEAD_DIM
A_KV_W = A_KV_HEADS * HEAD_DIM
B_GW = B_HPG * HEAD_DIM
B_W = B_GW * len(B_GROUPS)
C_Q_W = C_HEADS * C_HEAD_DIM
QKV_W = A_Q_W + 2 * A_KV_W + 3 * B_W + C_Q_W
EPS = 1e-6
NEG_INF = -1e30
LOG2E = math.log2(math.e)

LANES = 128
VMEM_LIMIT = 56 * 1024 * 1024
A_SLABS = A_Q_W // LANES
B_SLABS = B_W // LANES
G_SLABS = B_GW // LANES
CLASS_STRIDE = 4

TOKEN_TILE = 1024
PROJ_SUBTILES = 2
MERGE_SUBTILES = 4
FFN_SUBTILES = 4
FFN_CHUNK = 256
ATTN_TQ = 128
ATTN_UNROLL = 8

BF16 = jnp.bfloat16
F32 = jnp.float32

A_HEAD_ORDER = (0, 4, 1, 5, 2, 6, 3, 7)


def _const_spec(shape):
    nd = len(shape)
    return pl.BlockSpec(shape, lambda *_: (0,) * nd, pipeline_mode=pl.Buffered(1))


def _params(n_grid):
    return pltpu.CompilerParams(dimension_semantics=("arbitrary",) * n_grid,
                                vmem_limit_bytes=VMEM_LIMIT)


def _rmsnorm(x, g):
    return x * lax.rsqrt(jnp.mean(x * x, axis=-1, keepdims=True) + EPS) * g


def _dot(a, b):
    return jnp.dot(a, b, preferred_element_type=F32)


def _dot_nt(a, b):
    return lax.dot_general(a, b, (((1,), (1,)), ((), ())), preferred_element_type=F32)


def _x_specs(xs, tm):
    if len(xs) == 1:
        return [pl.BlockSpec((None, tm, D_MODEL), lambda i, j: (i, j, 0))]
    b0 = xs[0].shape[0]
    last_j = xs[0].shape[1] // tm - 1
    first = pl.BlockSpec((None, tm, D_MODEL),
                         lambda i, j: (jnp.minimum(i, b0 - 1), jnp.where(i < b0, j, last_j), 0))
    second = pl.BlockSpec((None, tm, D_MODEL),
                          lambda i, j: (jnp.maximum(i - b0, 0), jnp.where(i < b0, 0, j), 0))
    return [first, second]


def _read_x(x_refs, split, rows):
    if len(x_refs) == 1:
        return x_refs[0][rows, :]
    return jnp.where(pl.program_id(0) < split, x_refs[0][rows, :], x_refs[1][rows, :])


def _memkv_kernel(mem_ref, g_ref, w_ref, o_ref):
    h = _rmsnorm(mem_ref[...], g_ref[...]).astype(BF16)
    o_ref[...] = _dot(h, w_ref[...]).astype(BF16)


def _memkv_call(mem, g_mem, w_mem_kv):
    b = mem.shape[0]
    return pl.pallas_call(
        _memkv_kernel,
        grid=(DEPTH, b),
        in_specs=[
            pl.BlockSpec((None, N_MEM, D_MODEL), lambda l, i: (i, 0, 0)),
            pl.BlockSpec((None, 1, D_MODEL), lambda l, i: (l, 0, 0)),
            pl.BlockSpec((None, D_MODEL, 2 * C_Q_W), lambda l, i: (l, 0, 0)),
        ],
        out_specs=pl.BlockSpec((None, None, N_MEM, 2 * C_Q_W), lambda l, i: (l, i, 0, 0)),
        out_shape=jax.ShapeDtypeStruct((DEPTH, b, N_MEM, 2 * C_Q_W), BF16),
        compiler_params=_params(2),
        name="memkv",
    )(mem, g_mem, w_mem_kv)


def _rope_tile(y, c, s_up, s_dn):
    return (y * c + pltpu.roll(y, LANES - ROPE_HALF, 1) * s_up + pltpu.roll(y, ROPE_HALF, 1) * s_dn)


def _proj_kernel(*refs, n_x, split, n_sub):
    x_refs = refs[:n_x]
    (g_ref, w_ref, rope_ref, mkv_ref, qa_ref, ka_ref, va_ref, qb_ref, kb_ref, vb_ref, oc_ref) = refs[n_x:]
    sub_rows = x_refs[0].shape[0] // n_sub
    for sub in range(n_sub):
        rows = slice(sub * sub_rows, (sub + 1) * sub_rows)
        h = _rmsnorm(_read_x(x_refs, split, rows), g_ref[...]).astype(BF16)

        def project(col0, width, store, rope_base):
            y = _dot(h, w_ref[:, col0:col0 + width])
            for t in range(width // LANES):
                y_t = y[:, t * LANES:(t + 1) * LANES]
                if rope_base is not None:
                    y_t = _rope_tile(y_t, rope_ref[rope_base, rows, :], rope_ref[rope_base + 1, rows, :],
                                     rope_ref[rope_base + 2, rows, :])
                store(t, y_t.astype(BF16))

        def slab_store(ref):
            def store(t, val):
                ref[t, rows, :] = val
            return store

        def flat_store(ref):
            def store(t, val):
                ref[rows, t * LANES:(t + 1) * LANES] = val
            return store

        rope_q, rope_k = 0, 3
        col = 0
        project(col, A_Q_W, slab_store(qa_ref), rope_q); col += A_Q_W
        project(col, A_KV_W, flat_store(ka_ref), rope_k); col += A_KV_W
        project(col, A_KV_W, flat_store(va_ref), None); col += A_KV_W
        project(col, B_W, slab_store(qb_ref), rope_q); col += B_W
        project(col, B_W, slab_store(kb_ref), rope_k); col += B_W
        project(col, B_W, slab_store(vb_ref), None); col += B_W

        qc = (_dot(h, w_ref[:, col:col + C_Q_W]) * (C_HEAD_DIM ** -0.5 * LOG2E)).astype(BF16)
        for hd in range(C_HEADS):
            lo, hi = hd * C_HEAD_DIM, (hd + 1) * C_HEAD_DIM
            s = _dot_nt(qc[:, lo:hi], mkv_ref[:, lo:hi])
            m = jnp.max(s, axis=-1, keepdims=True)
            p = jnp.exp2(s - m)
            denom = jnp.sum(p, axis=-1, keepdims=True)
            o = _dot(p.astype(BF16), mkv_ref[:, C_Q_W + lo:C_Q_W + hi])
            oc_ref[rows, lo:hi] = (o * (1.0 / denom)).astype(BF16)


def _proj_call(xs, g_pre, w_qkv, rope_tabs, mkv, layer):
    b = sum(x.shape[0] for x in xs)
    s = xs[0].shape[1]
    tm = TOKEN_TILE
    tok = lambda w: pl.BlockSpec((None, tm, w), lambda i, j: (i, j, 0))
    slab = lambda n: pl.BlockSpec((None, n, tm, LANES), lambda i, j: (i, 0, j, 0))
    slab_shape = lambda n: jax.ShapeDtypeStruct((b, n, s, LANES), BF16)
    flat_shape = lambda w: jax.ShapeDtypeStruct((b, s, w), BF16)
    return pl.pallas_call(
        functools.partial(_proj_kernel, n_x=len(xs), split=xs[0].shape[0], n_sub=PROJ_SUBTILES),
        grid=(b, s // tm),
        in_specs=_x_specs(xs, tm) + [
            _const_spec((1, D_MODEL)),
            _const_spec((D_MODEL, QKV_W)),
            pl.BlockSpec((6, tm, LANES), lambda i, j: (0, j, 0)),
            pl.BlockSpec((None, None, N_MEM, 2 * C_Q_W), lambda i, j: (layer, i, 0, 0)),
        ],
        out_specs=[slab(A_SLABS), tok(A_KV_W), tok(A_KV_W), slab(B_SLABS), slab(B_SLABS), slab(B_SLABS),
                   tok(C_Q_W)],
        out_shape=[slab_shape(A_SLABS), flat_shape(A_KV_W), flat_shape(A_KV_W),
                   slab_shape(B_SLABS), slab_shape(B_SLABS), slab_shape(B_SLABS), flat_shape(C_Q_W)],
        compiler_params=_params(2),
        name="proj",
    )(*xs, g_pre, w_qkv, rope_tabs, mkv)


def _head_lane_masks():
    lane = lax.broadcasted_iota(jnp.int32, (1, LANES), 1)
    lo = (lane < HEAD_DIM).astype(BF16)
    return lo, (1 - lo).astype(BF16)


def _band_bias(tq, kw, q0_minus_k0, radius):
    d = (lax.broadcasted_iota(jnp.int32, (tq, kw), 0) + q0_minus_k0
         - lax.broadcasted_iota(jnp.int32, (tq, kw), 1))
    return jnp.where(jnp.abs(d) <= radius, 0.0, NEG_INF).astype(F32)


def _attend(q_t, k_pair, v_pair, bias, lane_is_lo, sinks):
    kw = k_pair.shape[0] // 2
    s = _dot_nt(q_t, k_pair)
    ms, ps = [], []
    for half in range(2):
        s_h = (s[:, half * kw:(half + 1) * kw] + bias).astype(BF16)
        m_h = jnp.broadcast_to(jnp.max(s_h, axis=-1, keepdims=True), (s_h.shape[0], LANES))
        if sinks is not None:
            m_h = jnp.maximum(m_h, sinks[half].astype(BF16))
        ps.append(jnp.exp2(s_h - jnp.concatenate([m_h] * (kw // LANES), axis=1)))
        ms.append(m_h.astype(F32))
    m = jnp.where(lane_is_lo, ms[0], ms[1])
    r = _dot(jnp.concatenate(ps, axis=1), v_pair)
    denom = r[:, LANES:]
    if sinks is not None:
        denom = denom + jnp.exp2(jnp.where(lane_is_lo, sinks[0], sinks[1]) - m)
    return r[:, :LANES] * (1.0 / denom), m, denom


def _fill_bias_table(bias_ref, tq, kw, radius):
    for n in range(bias_ref.shape[0]):
        bias_ref[n] = _band_bias(tq, kw, n * radius, radius)


def _pair_kv(k, v, lo, hi):
    k_pair = jnp.concatenate([k * lo, k * hi], axis=0)
    v_pair = jnp.concatenate([jnp.concatenate([v * lo, jnp.broadcast_to(lo, v.shape)], axis=1),
                              jnp.concatenate([v * hi, jnp.broadcast_to(hi, v.shape)], axis=1)], axis=0)
    return k_pair, v_pair


def _attend_rows(q_t, k, v, bias, lo, hi, lane_is_lo):
    tq, kw = q_t.shape[0], k.shape[0]
    s = _dot_nt(jnp.concatenate([q_t * lo, q_t * hi], axis=0), k)
    s = (s.reshape(2, tq, kw) + bias[None]).reshape(2 * tq, kw).astype(BF16)
    m = jnp.broadcast_to(jnp.max(s, axis=-1, keepdims=True), (2 * tq, LANES))
    p = jnp.exp2(s - jnp.concatenate([m] * (kw // LANES), axis=1))
    r = _dot(p, jnp.concatenate([v, jnp.ones(v.shape, BF16)], axis=1))
    o = jnp.where(lane_is_lo, r[:tq, :LANES], r[tq:, :LANES])
    denom = jnp.where(lane_is_lo, r[:tq, LANES:], r[tq:, LANES:])
    m = m.astype(F32)
    return o * (1.0 / denom), jnp.where(lane_is_lo, m[:tq], m[tq:]), denom


def _attn_a_kernel(sink_ref, q_ref, k_ref, v_ref, o_ref, bias_ref, *, tq, kw, seq):
    lo, hi = _head_lane_masks()
    lane_is_lo = lax.broadcasted_iota(jnp.int32, (1, LANES), 1) < HEAD_DIM
    sinks = [jnp.full((tq, LANES), sink_ref[h] * LOG2E, F32) for h in range(A_HEADS)]
    _fill_bias_table(bias_ref, tq, kw, A_RADIUS)

    def block(i, carry):
        q0 = pl.multiple_of(i * tq, tq)
        start = pl.multiple_of(jnp.clip(q0 - A_RADIUS, 0, seq - kw), LANES)
        bias = bias_ref[(q0 - start) // A_RADIUS]
        k_pair, v_pair = _pair_kv(k_ref[pl.ds(start, kw), :], v_ref[pl.ds(start, kw), :], lo, hi)
        for t in range(A_SLABS):
            q_t = q_ref[t, pl.ds(q0, tq), :]
            o, _, _ = _attend(q_t, k_pair, v_pair, bias, lane_is_lo,
                              (sinks[A_HEAD_ORDER[2 * t]], sinks[A_HEAD_ORDER[2 * t + 1]]))
            o_ref[t, pl.ds(q0, tq), :] = o.astype(BF16)
        return carry

    lax.fori_loop(0, seq // tq, block, 0, unroll=ATTN_UNROLL)


def _attn_a_call(sink, qa, ka, va):
    b, _, s, _ = qa.shape
    tq = ATTN_TQ
    kw = tq + 2 * A_RADIUS
    slab = pl.BlockSpec((None, A_SLABS, s, LANES), lambda i: (i, 0, 0, 0))
    kv = pl.BlockSpec((None, s, A_KV_W), lambda i: (i, 0, 0))
    return pl.pallas_call(
        functools.partial(_attn_a_kernel, tq=tq, kw=kw, seq=s),
        grid=(b,),
        in_specs=[pl.BlockSpec(memory_space=pltpu.SMEM), slab, kv, kv],
        out_specs=slab,
        out_shape=jax.ShapeDtypeStruct((b, A_SLABS, s, LANES), BF16),
        scratch_shapes=[pltpu.VMEM((3, tq, kw), F32)],
        compiler_params=_params(1),
        name="attn_a",
    )(sink, qa, ka, va)


def _attn_b_kernel(q_ref, k_ref, v_ref, o_ref, lse_ref, bias_ref, *staging, dil, length, radius, tq, kw):
    lo, hi = _head_lane_masks()
    lane_is_lo = lax.broadcasted_iota(jnp.int32, (1, LANES), 1) < HEAD_DIM
    nblk = length // tq
    seq = length * dil
    per = seq // CLASS_STRIDE
    levels = {1: 0, CLASS_STRIDE: 1, CLASS_STRIDE * CLASS_STRIDE: 2}[dil]
    _fill_bias_table(bias_ref, tq, kw, radius)

    def quarters(fn):
        for t in range(G_SLABS):
            for c0 in range(CLASS_STRIDE):
                fn(t, slice(c0 * per, (c0 + 1) * per), pl.ds(c0, per, stride=CLASS_STRIDE))

    if levels == 1:
        chunks = [(slice(c * per, (c + 1) * per), pl.ds(c, per, stride=CLASS_STRIDE)) for c in range(CLASS_STRIDE)]
    elif levels == 2:
        sub = per // CLASS_STRIDE
        chunks = [(slice(c0 * per + c1 * sub, c0 * per + (c1 + 1) * sub),
                   pl.ds(c0 * per + c1, sub, stride=CLASS_STRIDE))
                  for c0 in range(CLASS_STRIDE) for c1 in range(CLASS_STRIDE)]

    if levels == 0:
        qc_ref, kc_ref, vc_ref, oc_ref, lc_ref = q_ref, k_ref, v_ref, o_ref, lse_ref
    else:
        nat_ref, qc_ref, kc_ref, vc_ref, oc_ref, lc_ref = staging[:6]
        last_ref = staging[6] if levels == 2 else nat_ref

        def to_class_major(dst_ref, src_ref):
            nat_ref[...] = src_ref[...].astype(F32)
            if levels == 2:
                def level1(t, rows, strided):
                    last_ref[t, rows, :] = nat_ref[t, strided, :]
                quarters(level1)
            for t in range(G_SLABS):
                for rows, strided in chunks:
                    dst_ref[t, rows, :] = last_ref[t, strided, :].astype(BF16)

        to_class_major(qc_ref, q_ref)
        to_class_major(kc_ref, k_ref)
        to_class_major(vc_ref, v_ref)

    def unit(u, carry):
        c = u // nblk if nblk > 1 else u
        i = u - c * nblk if nblk > 1 else 0
        q0 = i * tq
        start = jnp.clip(q0 - radius, 0, length - kw)
        bias = bias_ref[(q0 - start) // radius] if nblk > 1 else bias_ref[0]
        q_rows = pl.ds(pl.multiple_of(u * tq, tq), tq)
        win = pl.ds(pl.multiple_of(c * length + start, radius), kw)
        for t in range(G_SLABS):
            q_t, k, v = qc_ref[t, q_rows, :], kc_ref[t, win, :], vc_ref[t, win, :]
            if kw >= 2 * LANES:
                o, m, denom = _attend_rows(q_t, k, v, bias, lo, hi, lane_is_lo)
            else:
                o, m, denom = _attend(q_t, *_pair_kv(k, v, lo, hi), bias, lane_is_lo, None)
            oc_ref[t, q_rows, :] = o.astype(oc_ref.dtype)
            lc_ref[t, q_rows, :] = m + jnp.log2(denom)
        return carry

    lax.fori_loop(0, dil * nblk, unit, 0, unroll=ATTN_UNROLL)

    if levels > 0:
        def to_natural(dst_ref, src_ref):
            for t in range(G_SLABS):
                for rows, strided in chunks:
                    (last_ref if levels == 2 else dst_ref)[t, strided, :] = src_ref[t, rows, :]
            if levels == 2:
                def level1(t, rows, strided):
                    dst_ref[t, strided, :] = last_ref[t, rows, :]
                quarters(level1)

        to_natural(nat_ref, oc_ref)
        o_ref[...] = nat_ref[...].astype(BF16)
        to_natural(lse_ref, lc_ref)


def _attn_b_call(q, k, v, group):
    window, dil = B_GROUPS[group]
    b, _, s, _ = q.shape
    length = s // dil
    radius = window // (2 * dil)
    tq = min(ATTN_TQ, length)
    kw = min(tq + 2 * radius, length)
    in_slab = pl.BlockSpec((None, G_SLABS, s, LANES), lambda i: (i, group, 0, 0))
    out_slab = pl.BlockSpec((None, G_SLABS, s, LANES), lambda i: (i, 0, 0, 0))
    slabs = lambda dtype: pltpu.VMEM((G_SLABS, s, LANES), dtype)
    staging = []
    if dil > 1:
        staging = [slabs(F32), slabs(BF16), slabs(BF16), slabs(BF16), slabs(F32), slabs(F32)]
    if dil > CLASS_STRIDE:
        staging.append(slabs(F32))
    n_bias = 3 if length > tq else 1
    return pl.pallas_call(
        functools.partial(_attn_b_kernel, dil=dil, length=length, radius=radius, tq=tq, kw=kw),
        grid=(b,),
        in_specs=[in_slab, in_slab, in_slab],
        out_specs=[out_slab, out_slab],
        out_shape=[jax.ShapeDtypeStruct((b, G_SLABS, s, LANES), BF16),
                   jax.ShapeDtypeStruct((b, G_SLABS, s, LANES), F32)],
        scratch_shapes=[pltpu.VMEM((n_bias, tq, kw), F32)] + staging,
        compiler_params=_params(1),
        name=f"attn_b_dil{dil}",
    )(q, k, v)


def _lanes(ref, rows):
    return jnp.concatenate([ref[t, rows, :] for t in range(ref.shape[0])], axis=-1)


def _merge_kernel(*refs, n_x, split, n_sub):
    x_refs = refs[:n_x]
    (oa_ref, ob0_ref, ob1_ref, ob2_ref, l0_ref, l1_ref, l2_ref, oc_ref,
     g_pre_ref, g_post_ref, w_gate_ref, w_oa_ref, w_ob_ref, w_oc_ref, w_out_ref, out_ref) = refs[n_x:]
    sub_rows = out_ref.shape[0] // n_sub
    for sub in range(n_sub):
        rows = slice(sub * sub_rows, (sub + 1) * sub_rows)
        x = _read_x(x_refs, split, rows)
        h = _rmsnorm(x, g_pre_ref[...]).astype(BF16)

        l0, l1, l2 = _lanes(l0_ref, rows), _lanes(l1_ref, rows), _lanes(l2_ref, rows)
        m = jnp.maximum(jnp.maximum(l0, l1), l2)
        e0, e1, e2 = jnp.exp2(l0 - m), jnp.exp2(l1 - m), jnp.exp2(l2 - m)
        num = (e0 * _lanes(ob0_ref, rows).astype(F32) + e1 * _lanes(ob1_ref, rows).astype(F32)
               + e2 * _lanes(ob2_ref, rows).astype(F32))
        ob = (num * (1.0 / (e0 + e1 + e2))).astype(BF16)

        merged = None
        for br, (o_br, w_ref) in enumerate(((_lanes(oa_ref, rows), w_oa_ref), (ob, w_ob_ref),
                                            (oc_ref[rows, :], w_oc_ref))):
            gate = jax.nn.sigmoid(_dot(h, w_gate_ref[:, br * D_MODEL:(br + 1) * D_MODEL]))
            term = gate * _dot(o_br, w_ref[...])
            merged = term if merged is None else merged + term
        z = _dot(merged.astype(BF16), w_out_ref[...])
        out_ref[rows, :] = x + _rmsnorm(z, g_post_ref[...])


def _merge_call(xs, oa, obs, lses, oc, g_pre, g_post, w_gate, w_oa, w_ob, w_oc, w_out):
    b, _, s, _ = oa.shape
    tm = TOKEN_TILE
    tok = lambda w: pl.BlockSpec((None, tm, w), lambda i, j: (i, j, 0))
    slab = lambda n: pl.BlockSpec((None, n, tm, LANES), lambda i, j: (i, 0, j, 0))
    g = slab(G_SLABS)
    return pl.pallas_call(
        functools.partial(_merge_kernel, n_x=len(xs), split=xs[0].shape[0], n_sub=MERGE_SUBTILES),
        grid=(b, s // tm),
        in_specs=_x_specs(xs, tm) + [
            slab(A_SLABS), g, g, g, g, g, g, tok(C_Q_W),
            _const_spec((1, D_MODEL)), _const_spec((1, D_MODEL)),
            _const_spec((D_MODEL, 3 * D_MODEL)),
            _const_spec((A_Q_W, D_MODEL)), _const_spec((B_GW, D_MODEL)), _const_spec((C_Q_W, D_MODEL)),
            _const_spec((D_MODEL, D_MODEL))],
        out_specs=tok(D_MODEL),
        out_shape=jax.ShapeDtypeStruct((b, s, D_MODEL), F32),
        compiler_params=_params(2),
        name="merge",
    )(*xs, oa, *obs, *lses, oc, g_pre, g_post, w_gate, w_oa, w_ob, w_oc, w_out)


def _ffn_kernel(x_ref, g_pre_ref, g_post_ref, w_in_ref, w_out_ref, out_ref, act_ref, *, n_sub):
    sub_rows = x_ref.shape[0] // n_sub
    for sub in range(n_sub):
        rows = slice(sub * sub_rows, (sub + 1) * sub_rows)
        x = x_ref[rows, :]
        h = _rmsnorm(x, g_pre_ref[...]).astype(BF16)
        for c0 in range(0, FFN_HIDDEN, FFN_CHUNK):
            g = _dot(h, w_in_ref[:, c0:c0 + FFN_CHUNK])
            u = _dot(h, w_in_ref[:, FFN_HIDDEN + c0:FFN_HIDDEN + c0 + FFN_CHUNK])
            act_ref[rows, c0:c0 + FFN_CHUNK] = (g * jax.nn.sigmoid(g) * u).astype(BF16)
        y = _dot(act_ref[rows, :], w_out_ref[...])
        out_ref[rows, :] = x + _rmsnorm(y, g_post_ref[...])


def _ffn_call(x, g_pre, g_post, w_in, w_out, batch_offset, batch):
    s = x.shape[1]
    tm = TOKEN_TILE
    return pl.pallas_call(
        functools.partial(_ffn_kernel, n_sub=FFN_SUBTILES),
        grid=(batch, s // tm),
        in_specs=[pl.BlockSpec((None, tm, D_MODEL), lambda i, j: (i + batch_offset, j, 0)),
                  _const_spec((1, D_MODEL)), _const_spec((1, D_MODEL)),
                  _const_spec((D_MODEL, 2 * FFN_HIDDEN)), _const_spec((FFN_HIDDEN, D_MODEL))],
        out_specs=pl.BlockSpec((None, tm, D_MODEL), lambda i, j: (i, j, 0)),
        out_shape=jax.ShapeDtypeStruct((batch, s, D_MODEL), F32),
        scratch_shapes=[pltpu.VMEM((tm, FFN_HIDDEN), BF16)],
        compiler_params=_params(2),
        name="ffn",
    )(x, g_pre, g_post, w_in, w_out)


def _rope_tables(seq):
    inv_freq = ROPE_THETA ** (-jnp.arange(0, ROPE_DIMS, 2, dtype=F32) / ROPE_DIMS)
    ang = jnp.arange(seq, dtype=F32)[:, None] * inv_freq[None, :]
    cos, sin = jnp.cos(ang), jnp.sin(ang)
    pad = jnp.zeros((seq, HEAD_DIM - ROPE_DIMS), F32)
    zero = jnp.zeros_like(sin)
    c = jnp.concatenate([cos, cos, pad + 1.0], axis=-1)
    s_up = jnp.concatenate([-sin, zero, pad], axis=-1)
    s_dn = jnp.concatenate([zero, sin, pad], axis=-1)
    tabs = jnp.stack([jnp.tile(t, (1, LANES // HEAD_DIM)) for t in (c, s_up, s_dn)])
    return jnp.concatenate([tabs * (HEAD_DIM ** -0.5 * LOG2E), tabs], axis=0)


def _layer_weights(l, norm_mix_pre, norm_mix_post, w_in, sink_a, w_o_a, w_o_b, w_o_c, w_out,
                   norm_ffn_pre, norm_ffn_post, w_ffn_in, w_ffn_out):
    order = jnp.array(A_HEAD_ORDER)
    w_in_l = w_in[l]
    w_qa = w_in_l[:, :A_Q_W].reshape(D_MODEL, A_HEADS, HEAD_DIM)[:, order].reshape(D_MODEL, A_Q_W)
    w_qkv = jnp.concatenate([w_qa, w_in_l[:, A_Q_W:QKV_W]], axis=1).astype(BF16)
    w_oa = w_o_a[l].reshape(A_HEADS, HEAD_DIM, D_MODEL)[order].reshape(A_Q_W, D_MODEL).astype(BF16)
    row = lambda g: g[l].reshape(1, D_MODEL)
    return dict(
        g_mix_pre=row(norm_mix_pre), g_mix_post=row(norm_mix_post), w_qkv=w_qkv,
        w_gate=w_in_l[:, QKV_W:].astype(BF16), sink=sink_a[l],
        w_oa=w_oa, w_ob=w_o_b[l].astype(BF16), w_oc=w_o_c[l].astype(BF16), w_out=w_out[l].astype(BF16),
        g_ffn_pre=row(norm_ffn_pre), g_ffn_post=row(norm_ffn_post),
        w_ffn_in=w_ffn_in[l].astype(BF16), w_ffn_out=w_ffn_out[l].astype(BF16))


@jax.jit
def kernel(x_prompt, x_sample, mem_prompt, mem_sample, norm_mix_pre, norm_mix_post, norm_mem, w_in, sink_a,
           w_mem_kv, w_o_a, w_o_b, w_o_c, w_out, norm_ffn_pre, norm_ffn_post, w_ffn_in, w_ffn_out):
    layers = [_layer_weights(l, norm_mix_pre, norm_mix_post, w_in, sink_a, w_o_a, w_o_b, w_o_c, w_out,
                             norm_ffn_pre, norm_ffn_post, w_ffn_in, w_ffn_out) for l in range(DEPTH)]
    rope_tabs = _rope_tables(x_prompt.shape[1])
    mkv = _memkv_call(jnp.concatenate([mem_prompt, mem_sample], axis=0),
                      norm_mem.reshape(DEPTH, 1, D_MODEL), w_mem_kv.astype(BF16))
    batches = (x_prompt.shape[0], x_sample.shape[0])
    xs = (x_prompt, x_sample)
    for l, w in enumerate(layers):
        qa, ka, va, qb, kb, vb, oc = _proj_call(xs, w["g_mix_pre"], w["w_qkv"], rope_tabs, mkv, l)
        oa = _attn_a_call(w["sink"], qa, ka, va)
        obs, lses = zip(*[_attn_b_call(qb, kb, vb, group) for group in range(len(B_GROUPS))])
        x = _merge_call(xs, oa, obs, lses, oc, w["g_mix_pre"], w["g_mix_post"], w["w_gate"],
                        w["w_oa"], w["w_ob"], w["w_oc"], w["w_out"])
        ffn = functools.partial(_ffn_call, x, w["g_ffn_pre"], w["g_ffn_post"], w["w_ffn_in"], w["w_ffn_out"])
        if l + 1 < DEPTH:
            xs = (ffn(0, sum(batches)),)
    return (ffn(0, batches[0]), ffn(batches[0], batches[1]))
```

```python
import functools
import math

import jax
import jax.numpy as jnp
from jax import lax
from jax.experimental import pallas as pl
from jax.experimental.pallas import tpu as pltpu

D_MODEL = 1024
DEPTH = 4
HEAD_DIM = 64
A_HEADS = 8
A_KV_HEADS = 2
A_RADIUS = 128
B_GROUPS = ((128, 1), (512, 4), (2048, 16))
B_HPG = 4
C_HEADS = 4
C_HEAD_DIM = 128
N_MEM = 256
ROPE_THETA = 500000.0
ROPE_DIMS = HEAD_DIM // 4
ROPE_HALF = ROPE_DIMS // 2
FFN_HIDDEN = 2816
A_Q_W = A_HEADS * HEAD_DIM
A_KV_W = A_KV_HEADS * HEAD_DIM
B_GW = B_HPG * HEAD_DIM
B_W = B_GW * len(B_GROUPS)
C_Q_W = C_HEADS * C_HEAD_DIM
QKV_W = A_Q_W + 2 * A_KV_W + 3 * B_W + C_Q_W
EPS = 1e-6
NEG_INF = -1e30
LOG2E = math.log2(math.e)

LANES = 128
VMEM_LIMIT = 56 * 1024 * 1024
A_SLABS = A_Q_W // LANES
G_SLABS = B_GW // LANES
CLASS_STRIDE = 4

TOKEN_TILE = 1024
PROJ_SUBTILES = 2
PROJ_STAGE_SLOTS = 12
MERGE_SUBTILES = 4
MERGE_STAGE_SLOTS = 12
FFN_SUBTILES = 4
FFN_CHUNK = 256
ATTN_TQ = 128
ATTN_UNROLL = 16

BF16 = jnp.bfloat16
F32 = jnp.float32

A_HEAD_ORDER = (0, 4, 1, 5, 2, 6, 3, 7)


def _const_spec(shape):
    nd = len(shape)
    return pl.BlockSpec(shape, lambda *_: (0,) * nd, pipeline_mode=pl.Buffered(1))


def _params(n_grid):
    return pltpu.CompilerParams(dimension_semantics=("arbitrary",) * n_grid,
                                vmem_limit_bytes=VMEM_LIMIT)


def _rmsnorm(x, g):
    return x * lax.rsqrt(jnp.mean(x * x, axis=-1, keepdims=True) + EPS) * g


def _dot(a, b):
    return jnp.dot(a, b, preferred_element_type=F32)


def _dot_nt(a, b):
    return lax.dot_general(a, b, (((1,), (1,)), ((), ())), preferred_element_type=F32)


def _x_specs(xs, tm):
    if len(xs) == 1:
        return [pl.BlockSpec((None, tm, D_MODEL), lambda i, j: (i, j, 0))]
    b0 = xs[0].shape[0]
    last_j = xs[0].shape[1] // tm - 1
    first = pl.BlockSpec((None, tm, D_MODEL),
                         lambda i, j: (jnp.minimum(i, b0 - 1), jnp.where(i < b0, j, last_j), 0))
    second = pl.BlockSpec((None, tm, D_MODEL),
                          lambda i, j: (jnp.maximum(i - b0, 0), jnp.where(i < b0, 0, j), 0))
    return [first, second]


def _class_shape(b, s, dil, dtype):
    return jax.ShapeDtypeStruct((b, G_SLABS, dil, s // dil, LANES), dtype)


def _class_spec(dil, tm):
    return pl.BlockSpec((None, G_SLABS, dil, tm // dil, LANES), lambda i, j: (i, 0, 0, j, 0))


def _read_x(x_refs, split, rows):
    if len(x_refs) == 1:
        return x_refs[0][rows, :]
    return jnp.where(pl.program_id(0) < split, x_refs[0][rows, :], x_refs[1][rows, :])


def _memkv_kernel(mem_ref, g_ref, w_ref, o_ref):
    h = _rmsnorm(mem_ref[...], g_ref[...]).astype(BF16)
    o_ref[...] = _dot(h, w_ref[...]).astype(BF16)


def _memkv_call(mem, g_mem, w_mem_kv):
    b = mem.shape[0]
    return pl.pallas_call(
        _memkv_kernel,
        grid=(DEPTH, b),
        in_specs=[
            pl.BlockSpec((None, N_MEM, D_MODEL), lambda l, i: (i, 0, 0)),
            pl.BlockSpec((None, 1, D_MODEL), lambda l, i: (l, 0, 0)),
            pl.BlockSpec((None, D_MODEL, 2 * C_Q_W), lambda l, i: (l, 0, 0)),
        ],
        out_specs=pl.BlockSpec((None, None, N_MEM, 2 * C_Q_W), lambda l, i: (l, i, 0, 0)),
        out_shape=jax.ShapeDtypeStruct((DEPTH, b, N_MEM, 2 * C_Q_W), BF16),
        compiler_params=_params(2),
        name="memkv",
    )(mem, g_mem, w_mem_kv)


def _rope_tile(y, c, s_up, s_dn):
    return (y * c + pltpu.roll(y, LANES - ROPE_HALF, 1) * s_up + pltpu.roll(y, ROPE_HALF, 1) * s_dn)


def _proj_kernel(*refs, n_x, split, n_sub):
    x_refs = refs[:n_x]
    (g_ref, w_ref, rope_ref, mkv_ref, qa_ref, ka_ref, va_ref) = refs[n_x:n_x + 7]
    b_refs = refs[n_x + 7:n_x + 16]
    oc_ref, stage_ref = refs[n_x + 16:]
    sub_rows = x_refs[0].shape[0] // n_sub
    n_stage = 0
    for sub in range(n_sub):
        rows = slice(sub * sub_rows, (sub + 1) * sub_rows)
        h = _rmsnorm(_read_x(x_refs, split, rows), g_ref[...]).astype(BF16)

        def project(col0, width, store, rope_base):
            y = _dot(h, w_ref[:, col0:col0 + width])
            for t in range(width // LANES):
                y_t = y[:, t * LANES:(t + 1) * LANES]
                if rope_base is not None:
                    y_t = _rope_tile(y_t, rope_ref[rope_base, rows, :], rope_ref[rope_base + 1, rows, :],
                                     rope_ref[rope_base + 2, rows, :])
                store(t, y_t)

        def slab_store(ref):
            def store(t, val):
                ref[t, rows, :] = val.astype(BF16)
            return store

        def flat_store(ref):
            def store(t, val):
                ref[rows, t * LANES:(t + 1) * LANES] = val.astype(BF16)
            return store

        def class_store(group_refs):
            def store(t, val):
                nonlocal n_stage
                (_, dil), ref = B_GROUPS[t // G_SLABS], group_refs[t // G_SLABS]
                n = sub_rows // dil
                dst = slice(sub * n, (sub + 1) * n)
                if dil == 1:
                    ref[t % G_SLABS, 0, dst, :] = val.astype(BF16)
                    return
                slot = stage_ref.at[n_stage % stage_ref.shape[0]]
                slot[...] = val
                if dil == CLASS_STRIDE:
                    n_stage += 1
                    for c in range(dil):
                        ref[t % G_SLABS, c, dst, :] = slot[pl.ds(c, n, stride=CLASS_STRIDE), :].astype(BF16)
                    return
                assert dil == CLASS_STRIDE * CLASS_STRIDE
                quarter = sub_rows // CLASS_STRIDE
                slot2 = stage_ref.at[(n_stage + 1) % stage_ref.shape[0]]
                n_stage += 2
                for c0 in range(CLASS_STRIDE):
                    slot2[c0 * quarter:(c0 + 1) * quarter, :] = slot[pl.ds(c0, quarter, stride=CLASS_STRIDE), :]
                for c0 in range(CLASS_STRIDE):
                    for c1 in range(CLASS_STRIDE):
                        ref[t % G_SLABS, CLASS_STRIDE * c1 + c0, dst, :] = (
                            slot2[pl.ds(c0 * quarter + c1, n, stride=CLASS_STRIDE), :].astype(BF16))
            return store

        rope_q, rope_k = 0, 3
        col = 0
        project(col, A_Q_W, slab_store(qa_ref), rope_q); col += A_Q_W
        project(col, A_KV_W, flat_store(ka_ref), rope_k); col += A_KV_W
        project(col, A_KV_W, flat_store(va_ref), None); col += A_KV_W
        project(col, B_W, class_store(b_refs[0:3]), rope_q); col += B_W
        project(col, B_W, class_store(b_refs[3:6]), rope_k); col += B_W
        project(col, B_W, class_store(b_refs[6:9]), None); col += B_W

        qc = (_dot(h, w_ref[:, col:col + C_Q_W]) * (C_HEAD_DIM ** -0.5 * LOG2E)).astype(BF16)
        for hd in range(C_HEADS):
            lo, hi = hd * C_HEAD_DIM, (hd + 1) * C_HEAD_DIM
            s = _dot_nt(qc[:, lo:hi], mkv_ref[:, lo:hi])
            m = jnp.max(s, axis=-1, keepdims=True)
            p = jnp.exp2(s - m)
            denom = jnp.sum(p, axis=-1, keepdims=True)
            o = _dot(p.astype(BF16), mkv_ref[:, C_Q_W + lo:C_Q_W + hi])
            oc_ref[rows, lo:hi] = (o * (1.0 / denom)).astype(BF16)


def _proj_call(xs, g_pre, w_qkv, rope_tabs, mkv, layer):
    b = sum(x.shape[0] for x in xs)
    s = xs[0].shape[1]
    tm = TOKEN_TILE
    tok = lambda w: pl.BlockSpec((None, tm, w), lambda i, j: (i, j, 0))
    slab = lambda n: pl.BlockSpec((None, n, tm, LANES), lambda i, j: (i, 0, j, 0))
    slab_shape = lambda n: jax.ShapeDtypeStruct((b, n, s, LANES), BF16)
    flat_shape = lambda w: jax.ShapeDtypeStruct((b, s, w), BF16)
    group_specs = [_class_spec(dil, tm) for _ in range(3) for _, dil in B_GROUPS]
    group_shapes = [_class_shape(b, s, dil, BF16) for _ in range(3) for _, dil in B_GROUPS]
    outs = pl.pallas_call(
        functools.partial(_proj_kernel, n_x=len(xs), split=xs[0].shape[0], n_sub=PROJ_SUBTILES),
        grid=(b, s // tm),
        in_specs=_x_specs(xs, tm) + [
            _const_spec((1, D_MODEL)),
            _const_spec((D_MODEL, QKV_W)),
            pl.BlockSpec((6, tm, LANES), lambda i, j: (0, j, 0)),
            pl.BlockSpec((None, None, N_MEM, 2 * C_Q_W), lambda i, j: (layer, i, 0, 0)),
        ],
        out_specs=[slab(A_SLABS), tok(A_KV_W), tok(A_KV_W)] + group_specs + [tok(C_Q_W)],
        out_shape=[slab_shape(A_SLABS), flat_shape(A_KV_W), flat_shape(A_KV_W)] + group_shapes
                  + [flat_shape(C_Q_W)],
        scratch_shapes=[pltpu.VMEM((PROJ_STAGE_SLOTS, tm // PROJ_SUBTILES, LANES), F32)],
        compiler_params=_params(2),
        name="proj",
    )(*xs, g_pre, w_qkv, rope_tabs, mkv)
    qa, ka, va = outs[:3]
    return qa, ka, va, outs[3:6], outs[6:9], outs[9:12], outs[12]


def _head_lane_masks():
    lane = lax.broadcasted_iota(jnp.int32, (1, LANES), 1)
    lo = (lane < HEAD_DIM).astype(BF16)
    return lo, (1 - lo).astype(BF16)


def _band_bias(tq, kw, q0_minus_k0, radius):
    d = (lax.broadcasted_iota(jnp.int32, (tq, kw), 0) + q0_minus_k0
         - lax.broadcasted_iota(jnp.int32, (tq, kw), 1))
    return jnp.where(jnp.abs(d) <= radius, 0.0, NEG_INF).astype(F32)


def _attend(q_t, k_pair, v_pair, bias, lane_is_lo, sinks):
    kw = k_pair.shape[0] // 2
    s = _dot_nt(q_t, k_pair)
    ms, ps = [], []
    for half in range(2):
        s_h = (s[:, half * kw:(half + 1) * kw] + bias).astype(BF16)
        m_h = jnp.broadcast_to(jnp.max(s_h, axis=-1, keepdims=True), (s_h.shape[0], LANES))
        if sinks is not None:
            m_h = jnp.maximum(m_h, sinks[half].astype(BF16))
        ps.append(jnp.exp2(s_h - jnp.concatenate([m_h] * (kw // LANES), axis=1)))
        ms.append(m_h.astype(F32))
    m = jnp.where(lane_is_lo, ms[0], ms[1])
    r = _dot(jnp.concatenate(ps, axis=1), v_pair)
    denom = r[:, LANES:]
    if sinks is not None:
        denom = denom + jnp.exp2(jnp.where(lane_is_lo, sinks[0], sinks[1]) - m)
    return r[:, :LANES] * (1.0 / denom), m, denom


def _fill_bias_table(bias_ref, tq, kw, radius):
    for n in range(bias_ref.shape[0]):
        bias_ref[n] = _band_bias(tq, kw, n * radius, radius)


def _pair_kv(k, v, lo, hi):
    k_pair = jnp.concatenate([k * lo, k * hi], axis=0)
    v_pair = jnp.concatenate([jnp.concatenate([v * lo, jnp.broadcast_to(lo, v.shape)], axis=1),
                              jnp.concatenate([v * hi, jnp.broadcast_to(hi, v.shape)], axis=1)], axis=0)
    return k_pair, v_pair


def _attend_rows(q_t, k, v, bias, lo, hi, lane_is_lo):
    tq, kw = q_t.shape[0], k.shape[0]
    s = _dot_nt(jnp.concatenate([q_t * lo, q_t * hi], axis=0), k)
    s = (s.reshape(2, tq, kw) + bias[None]).reshape(2 * tq, kw).astype(BF16)
    m = jnp.broadcast_to(jnp.max(s, axis=-1, keepdims=True), (2 * tq, LANES))
    p = jnp.exp2(s - jnp.concatenate([m] * (kw // LANES), axis=1))
    r = _dot(p, jnp.concatenate([v, jnp.ones(v.shape, BF16)], axis=1))
    o = jnp.where(lane_is_lo, r[:tq, :LANES], r[tq:, :LANES])
    denom = jnp.where(lane_is_lo, r[:tq, LANES:], r[tq:, LANES:])
    m = m.astype(F32)
    return o * (1.0 / denom), jnp.where(lane_is_lo, m[:tq], m[tq:]), denom


def _attn_a_kernel(sink_ref, q_ref, k_ref, v_ref, o_ref, bias_ref, *, tq, kw, seq):
    lo, hi = _head_lane_masks()
    lane_is_lo = lax.broadcasted_iota(jnp.int32, (1, LANES), 1) < HEAD_DIM
    sinks = [jnp.full((tq, LANES), sink_ref[h] * LOG2E, F32) for h in range(A_HEADS)]
    _fill_bias_table(bias_ref, tq, kw, A_RADIUS)

    def block(i, carry):
        q0 = pl.multiple_of(i * tq, tq)
        start = pl.multiple_of(jnp.clip(q0 - A_RADIUS, 0, seq - kw), LANES)
        bias = bias_ref[(q0 - start) // A_RADIUS]
        k_pair, v_pair = _pair_kv(k_ref[pl.ds(start, kw), :], v_ref[pl.ds(start, kw), :], lo, hi)
        for t in range(A_SLABS):
            q_t = q_ref[t, pl.ds(q0, tq), :]
            o, _, _ = _attend(q_t, k_pair, v_pair, bias, lane_is_lo,
                              (sinks[A_HEAD_ORDER[2 * t]], sinks[A_HEAD_ORDER[2 * t + 1]]))
            o_ref[t, pl.ds(q0, tq), :] = o.astype(BF16)
        return carry

    lax.fori_loop(0, seq // tq, block, 0, unroll=ATTN_UNROLL)


def _attn_a_call(sink, qa, ka, va):
    b, _, s, _ = qa.shape
    tq = ATTN_TQ
    kw = tq + 2 * A_RADIUS
    slab = pl.BlockSpec((None, A_SLABS, s, LANES), lambda i: (i, 0, 0, 0))
    kv = pl.BlockSpec((None, s, A_KV_W), lambda i: (i, 0, 0))
    return pl.pallas_call(
        functools.partial(_attn_a_kernel, tq=tq, kw=kw, seq=s),
        grid=(b,),
        in_specs=[pl.BlockSpec(memory_space=pltpu.SMEM), slab, kv, kv],
        out_specs=slab,
        out_shape=jax.ShapeDtypeStruct((b, A_SLABS, s, LANES), BF16),
        scratch_shapes=[pltpu.VMEM((3, tq, kw), F32)],
        compiler_params=_params(1),
        name="attn_a",
    )(sink, qa, ka, va)


def _attn_b_kernel(q_ref, k_ref, v_ref, o_ref, lse_ref, bias_ref, *, dil, length, radius, tq, kw):
    lo, hi = _head_lane_masks()
    lane_is_lo = lax.broadcasted_iota(jnp.int32, (1, LANES), 1) < HEAD_DIM
    nblk = length // tq
    _fill_bias_table(bias_ref, tq, kw, radius)

    def unit(u, carry):
        c = u // nblk if nblk > 1 else u
        i = u - c * nblk if nblk > 1 else 0
        q0 = pl.multiple_of(i * tq, tq)
        start = pl.multiple_of(jnp.clip(q0 - radius, 0, length - kw), radius)
        bias = bias_ref[(q0 - start) // radius] if nblk > 1 else bias_ref[0]
        q_rows, win = pl.ds(q0, tq), pl.ds(start, kw)
        for t in range(G_SLABS):
            q_t, k, v = q_ref[t, c, q_rows, :], k_ref[t, c, win, :], v_ref[t, c, win, :]
            if kw >= 2 * LANES:
                o, m, denom = _attend_rows(q_t, k, v, bias, lo, hi, lane_is_lo)
            else:
                o, m, denom = _attend(q_t, *_pair_kv(k, v, lo, hi), bias, lane_is_lo, None)
            o_ref[t, c, q_rows, :] = o.astype(BF16)
            lse_ref[t, c, q_rows, :] = m + jnp.log2(denom)
        return carry

    lax.fori_loop(0, dil * nblk, unit, 0, unroll=ATTN_UNROLL)


def _attn_b_call(q, k, v, group):
    window, dil = B_GROUPS[group]
    b, _, _, length, _ = q.shape
    radius = window // (2 * dil)
    tq = min(ATTN_TQ, length)
    kw = min(tq + 2 * radius, length)
    whole = pl.BlockSpec((None, G_SLABS, dil, length, LANES), lambda i: (i, 0, 0, 0, 0))
    return pl.pallas_call(
        functools.partial(_attn_b_kernel, dil=dil, length=length, radius=radius, tq=tq, kw=kw),
        grid=(b,),
        in_specs=[whole, whole, whole],
        out_specs=[whole, whole],
        out_shape=[_class_shape(b, length * dil, dil, BF16), _class_shape(b, length * dil, dil, F32)],
        scratch_shapes=[pltpu.VMEM((3 if length > tq else 1, tq, kw), F32)],
        compiler_params=_params(1),
        name=f"attn_b_dil{dil}",
    )(q, k, v)


def _lanes(ref, rows):
    return jnp.concatenate([ref[t, rows, :] for t in range(ref.shape[0])], axis=-1)


def _merge_kernel(*refs, n_x, split, n_sub):
    x_refs = refs[:n_x]
    (oa_ref, ob0_ref, ob1_ref, ob2_ref, l0_ref, l1_ref, l2_ref, oc_ref,
     g_pre_ref, g_post_ref, w_gate_ref, w_oa_ref, w_ob_ref, w_oc_ref, w_out_ref, out_ref, stage_ref) = refs[n_x:]
    sub_rows = out_ref.shape[0] // n_sub
    n_stage = 0

    def natural(ref, sub):
        nonlocal n_stage
        dil = ref.shape[1]
        n = sub_rows // dil
        src = slice(sub * n, (sub + 1) * n)
        tiles = []
        for t in range(G_SLABS):
            if dil == 1:
                tiles.append(ref[t, 0, src, :].astype(F32))
                continue
            slot = stage_ref.at[n_stage % stage_ref.shape[0]]
            if dil == CLASS_STRIDE:
                n_stage += 1
                for c in range(dil):
                    slot[pl.ds(c, n, stride=CLASS_STRIDE), :] = ref[t, c, src, :].astype(F32)
                tiles.append(slot[...])
                continue
            assert dil == CLASS_STRIDE * CLASS_STRIDE
            quarter = sub_rows // CLASS_STRIDE
            slot2 = stage_ref.at[(n_stage + 1) % stage_ref.shape[0]]
            n_stage += 2
            for c0 in range(CLASS_STRIDE):
                for c1 in range(CLASS_STRIDE):
                    slot[pl.ds(c0 * quarter + c1, n, stride=CLASS_STRIDE), :] = (
                        ref[t, CLASS_STRIDE * c1 + c0, src, :].astype(F32))
            for c0 in range(CLASS_STRIDE):
                slot2[pl.ds(c0, quarter, stride=CLASS_STRIDE), :] = slot[c0 * quarter:(c0 + 1) * quarter, :]
            tiles.append(slot2[...])
        return jnp.concatenate(tiles, axis=-1)

    for sub in range(n_sub):
        rows = slice(sub * sub_rows, (sub + 1) * sub_rows)
        x = _read_x(x_refs, split, rows)
        h = _rmsnorm(x, g_pre_ref[...]).astype(BF16)

        l0, l1, l2 = natural(l0_ref, sub), natural(l1_ref, sub), natural(l2_ref, sub)
        m = jnp.maximum(jnp.maximum(l0, l1), l2)
        e0, e1, e2 = jnp.exp2(l0 - m), jnp.exp2(l1 - m), jnp.exp2(l2 - m)
        num = e0 * natural(ob0_ref, sub) + e1 * natural(ob1_ref, sub) + e2 * natural(ob2_ref, sub)
        ob = (num * (1.0 / (e0 + e1 + e2))).astype(BF16)

        merged = None
        for br, (o_br, w_ref) in enumerate(((_lanes(oa_ref, rows), w_oa_ref), (ob, w_ob_ref),
                                            (oc_ref[rows, :], w_oc_ref))):
            gate = jax.nn.sigmoid(_dot(h, w_gate_ref[:, br * D_MODEL:(br + 1) * D_MODEL]))
            term = gate * _dot(o_br, w_ref[...])
            merged = term if merged is None else merged + term
        z = _dot(merged.astype(BF16), w_out_ref[...])
        out_ref[rows, :] = x + _rmsnorm(z, g_post_ref[...])


def _merge_call(xs, oa, obs, lses, oc, g_pre, g_post, w_gate, w_oa, w_ob, w_oc, w_out):
    b, _, s, _ = oa.shape
    tm = TOKEN_TILE
    tok = lambda w: pl.BlockSpec((None, tm, w), lambda i, j: (i, j, 0))
    slab = lambda n: pl.BlockSpec((None, n, tm, LANES), lambda i, j: (i, 0, j, 0))
    groups = [_class_spec(dil, tm) for _ in range(2) for _, dil in B_GROUPS]
    return pl.pallas_call(
        functools.partial(_merge_kernel, n_x=len(xs), split=xs[0].shape[0], n_sub=MERGE_SUBTILES),
        grid=(b, s // tm),
        in_specs=_x_specs(xs, tm) + [slab(A_SLABS)] + groups + [
            tok(C_Q_W),
            _const_spec((1, D_MODEL)), _const_spec((1, D_MODEL)),
            _const_spec((D_MODEL, 3 * D_MODEL)),
            _const_spec((A_Q_W, D_MODEL)), _const_spec((B_GW, D_MODEL)), _const_spec((C_Q_W, D_MODEL)),
            _const_spec((D_MODEL, D_MODEL))],
        out_specs=tok(D_MODEL),
        out_shape=jax.ShapeDtypeStruct((b, s, D_MODEL), F32),
        scratch_shapes=[pltpu.VMEM((MERGE_STAGE_SLOTS, tm // MERGE_SUBTILES, LANES), F32)],
        compiler_params=_params(2),
        name="merge",
    )(*xs, oa, *obs, *lses, oc, g_pre, g_post, w_gate, w_oa, w_ob, w_oc, w_out)


def _ffn_kernel(x_ref, g_pre_ref, g_post_ref, w_in_ref, w_out_ref, out_ref, act_ref, *, n_sub):
    sub_rows = x_ref.shape[0] // n_sub
    for sub in range(n_sub):
        rows = slice(sub * sub_rows, (sub + 1) * sub_rows)
        x = x_ref[rows, :]
        h = _rmsnorm(x, g_pre_ref[...]).astype(BF16)
        for c0 in range(0, FFN_HIDDEN, FFN_CHUNK):
            g = _dot(h, w_in_ref[:, c0:c0 + FFN_CHUNK])
            u = _dot(h, w_in_ref[:, FFN_HIDDEN + c0:FFN_HIDDEN + c0 + FFN_CHUNK])
            act_ref[rows, c0:c0 + FFN_CHUNK] = (g * jax.nn.sigmoid(g) * u).astype(BF16)
        y = _dot(act_ref[rows, :], w_out_ref[...])
        out_ref[rows, :] = x + _rmsnorm(y, g_post_ref[...])


def _ffn_call(x, g_pre, g_post, w_in, w_out, batch_offset, batch):
    s = x.shape[1]
    tm = TOKEN_TILE
    return pl.pallas_call(
        functools.partial(_ffn_kernel, n_sub=FFN_SUBTILES),
        grid=(batch, s // tm),
        in_specs=[pl.BlockSpec((None, tm, D_MODEL), lambda i, j: (i + batch_offset, j, 0)),
                  _const_spec((1, D_MODEL)), _const_spec((1, D_MODEL)),
                  _const_spec((D_MODEL, 2 * FFN_HIDDEN)), _const_spec((FFN_HIDDEN, D_MODEL))],
        out_specs=pl.BlockSpec((None, tm, D_MODEL), lambda i, j: (i, j, 0)),
        out_shape=jax.ShapeDtypeStruct((batch, s, D_MODEL), F32),
        scratch_shapes=[pltpu.VMEM((tm, FFN_HIDDEN), BF16)],
        compiler_params=_params(2),
        name="ffn",
    )(x, g_pre, g_post, w_in, w_out)


def _rope_tables(seq):
    inv_freq = ROPE_THETA ** (-jnp.arange(0, ROPE_DIMS, 2, dtype=F32) / ROPE_DIMS)
    ang = jnp.arange(seq, dtype=F32)[:, None] * inv_freq[None, :]
    cos, sin = jnp.cos(ang), jnp.sin(ang)
    pad = jnp.zeros((seq, HEAD_DIM - ROPE_DIMS), F32)
    zero = jnp.zeros_like(sin)
    c = jnp.concatenate([cos, cos, pad + 1.0], axis=-1)
    s_up = jnp.concatenate([-sin, zero, pad], axis=-1)
    s_dn = jnp.concatenate([zero, sin, pad], axis=-1)
    tabs = jnp.stack([jnp.tile(t, (1, LANES // HEAD_DIM)) for t in (c, s_up, s_dn)])
    return jnp.concatenate([tabs * (HEAD_DIM ** -0.5 * LOG2E), tabs], axis=0)


def _layer_weights(l, norm_mix_pre, norm_mix_post, w_in, sink_a, w_o_a, w_o_b, w_o_c, w_out,
                   norm_ffn_pre, norm_ffn_post, w_ffn_in, w_ffn_out):
    order = jnp.array(A_HEAD_ORDER)
    w_in_l = w_in[l]
    w_qa = w_in_l[:, :A_Q_W].reshape(D_MODEL, A_HEADS, HEAD_DIM)[:, order].reshape(D_MODEL, A_Q_W)
    w_qkv = jnp.concatenate([w_qa, w_in_l[:, A_Q_W:QKV_W]], axis=1).astype(BF16)
    w_oa = w_o_a[l].reshape(A_HEADS, HEAD_DIM, D_MODEL)[order].reshape(A_Q_W, D_MODEL).astype(BF16)
    row = lambda g: g[l].reshape(1, D_MODEL)
    return dict(
        g_mix_pre=row(norm_mix_pre), g_mix_post=row(norm_mix_post), w_qkv=w_qkv,
        w_gate=w_in_l[:, QKV_W:].astype(BF16), sink=sink_a[l],
        w_oa=w_oa, w_ob=w_o_b[l].astype(BF16), w_oc=w_o_c[l].astype(BF16), w_out=w_out[l].astype(BF16),
        g_ffn_pre=row(norm_ffn_pre), g_ffn_post=row(norm_ffn_post),
        w_ffn_in=w_ffn_in[l].astype(BF16), w_ffn_out=w_ffn_out[l].astype(BF16))


@jax.jit
def kernel(x_prompt, x_sample, mem_prompt, mem_sample, norm_mix_pre, norm_mix_post, norm_mem, w_in, sink_a,
           w_mem_kv, w_o_a, w_o_b, w_o_c, w_out, norm_ffn_pre, norm_ffn_post, w_ffn_in, w_ffn_out):
    layers = [_layer_weights(l, norm_mix_pre, norm_mix_post, w_in, sink_a, w_o_a, w_o_b, w_o_c, w_out,
                             norm_ffn_pre, norm_ffn_post, w_ffn_in, w_ffn_out) for l in range(DEPTH)]
    rope_tabs = _rope_tables(x_prompt.shape[1])
    mkv = _memkv_call(jnp.concatenate([mem_prompt, mem_sample], axis=0),
                      norm_mem.reshape(DEPTH, 1, D_MODEL), w_mem_kv.astype(BF16))
    batches = (x_prompt.shape[0], x_sample.shape[0])
    xs = (x_prompt, x_sample)
    for l, w in enumerate(layers):
        qa, ka, va, qb, kb, vb, oc = _proj_call(xs, w["g_mix_pre"], w["w_qkv"], rope_tabs, mkv, l)
        oa = _attn_a_call(w["sink"], qa, ka, va)
        obs, lses = zip(*[_attn_b_call(qb[g], kb[g], vb[g], g) for g in range(len(B_GROUPS))])
        x = _merge_call(xs, oa, obs, lses, oc, w["g_mix_pre"], w["g_mix_post"], w["w_gate"],
                        w["w_oa"], w["w_ob"], w["w_oc"], w["w_out"])
        ffn = functools.partial(_ffn_call, x, w["g_ffn_pre"], w["g_ffn_post"], w["w_ffn_in"], w["w_ffn_out"])
        if l + 1 < DEPTH:
            xs = (ffn(0, sum(batches)),)
    return (ffn(0, batches[0]), ffn(batches[0], batches[1]))
```

```python
import functools
import math

import jax
import jax.numpy as jnp
from jax import lax
from jax.experimental import pallas as pl
from jax.experimental.pallas import tpu as pltpu

D_MODEL = 1024
DEPTH = 4
HEAD_DIM = 64
A_HEADS = 8
A_KV_HEADS = 2
A_RADIUS = 128
B_GROUPS = ((128, 1), (512, 4), (2048, 16))
B_HPG = 4
C_HEADS = 4
C_HEAD_DIM = 128
N_MEM = 256
ROPE_THETA = 500000.0
ROPE_DIMS = HEAD_DIM // 4
ROPE_HALF = ROPE_DIMS // 2
FFN_HIDDEN = 2816
A_Q_W = A_HEADS * HEAD_DIM
A_KV_W = A_KV_HEADS * HEAD_DIM
B_GW = B_HPG * HEAD_DIM
B_W = B_GW * len(B_GROUPS)
C_Q_W = C_HEADS * C_HEAD_DIM
QKV_W = A_Q_W + 2 * A_KV_W + 3 * B_W + C_Q_W
EPS = 1e-6
NEG_INF = -1e30
LOG2E = math.log2(math.e)

LANES = 128
VMEM_LIMIT = 56 * 1024 * 1024
A_SLABS = A_Q_W // LANES
G_SLABS = B_GW // LANES
CLASS_STRIDE = 4

TOKEN_TILE = 1024
PROJ_SUBTILES = 2
PROJ_STAGE_SLOTS = 12
MERGE_SUBTILES = 4
MERGE_STAGE_SLOTS = 12
FFN_SUBTILES = 4
FFN_CHUNK = 256
MEMKV_BATCH = 4
ATTN_TQ = 128
ATTN_UNROLL = 16

BF16 = jnp.bfloat16
F32 = jnp.float32

A_HEAD_ORDER = (0, 4, 1, 5, 2, 6, 3, 7)


def _const_spec(shape):
    nd = len(shape)
    return pl.BlockSpec(shape, lambda *_: (0,) * nd, pipeline_mode=pl.Buffered(1))


def _params(n_grid):
    return pltpu.CompilerParams(dimension_semantics=("arbitrary",) * n_grid,
                                vmem_limit_bytes=VMEM_LIMIT)


def _rmsnorm(x, g):
    return x * lax.rsqrt(jnp.mean(x * x, axis=-1, keepdims=True) + EPS) * g


def _dot(a, b):
    return jnp.dot(a, b, preferred_element_type=F32)


def _dot_nt(a, b):
    return lax.dot_general(a, b, (((1,), (1,)), ((), ())), preferred_element_type=F32)


def _x_specs(xs, tm):
    if len(xs) == 1:
        return [pl.BlockSpec((None, tm, D_MODEL), lambda i, j: (i, j, 0))]
    b0 = xs[0].shape[0]
    last_j = xs[0].shape[1] // tm - 1
    first = pl.BlockSpec((None, tm, D_MODEL),
                         lambda i, j: (jnp.minimum(i, b0 - 1), jnp.where(i < b0, j, last_j), 0))
    second = pl.BlockSpec((None, tm, D_MODEL),
                          lambda i, j: (jnp.maximum(i - b0, 0), jnp.where(i < b0, 0, j), 0))
    return [first, second]


def _class_shape(b, s, dil, dtype):
    return jax.ShapeDtypeStruct((b, G_SLABS, dil, s // dil, LANES), dtype)


def _class_spec(dil, tm):
    return pl.BlockSpec((None, G_SLABS, dil, tm // dil, LANES), lambda i, j: (i, 0, 0, j, 0))


def _read_x(x_refs, split, rows):
    if len(x_refs) == 1:
        return x_refs[0][rows, :]
    return jnp.where(pl.program_id(0) < split, x_refs[0][rows, :], x_refs[1][rows, :])


def _memkv_kernel(mem_ref, g_ref, w_ref, o_ref):
    nb = mem_ref.shape[0]
    h = _rmsnorm(mem_ref[...].reshape(nb * N_MEM, D_MODEL), g_ref[...]).astype(BF16)
    o_ref[...] = _dot(h, w_ref[...]).astype(BF16).reshape(nb, N_MEM, 2 * C_Q_W)


def _memkv_call(mem, g_mem, w_mem_kv):
    b = mem.shape[0]
    nb = math.gcd(b, MEMKV_BATCH)
    return pl.pallas_call(
        _memkv_kernel,
        grid=(DEPTH, b // nb),
        in_specs=[
            pl.BlockSpec((nb, N_MEM, D_MODEL), lambda l, i: (i, 0, 0)),
            pl.BlockSpec((None, 1, D_MODEL), lambda l, i: (l, 0, 0)),
            pl.BlockSpec((None, D_MODEL, 2 * C_Q_W), lambda l, i: (l, 0, 0)),
        ],
        out_specs=pl.BlockSpec((None, nb, N_MEM, 2 * C_Q_W), lambda l, i: (l, i, 0, 0)),
        out_shape=jax.ShapeDtypeStruct((DEPTH, b, N_MEM, 2 * C_Q_W), BF16),
        compiler_params=_params(2),
        name="memkv",
    )(mem, g_mem, w_mem_kv)


def _rope_tile(y, c, s_up, s_dn):
    return (y * c + pltpu.roll(y, LANES - ROPE_HALF, 1) * s_up + pltpu.roll(y, ROPE_HALF, 1) * s_dn)


def _proj_kernel(*refs, n_x, split, n_sub):
    x_refs = refs[:n_x]
    (g_ref, w_ref, rope_ref, mkv_ref, qa_ref, ka_ref, va_ref) = refs[n_x:n_x + 7]
    b_refs = refs[n_x + 7:n_x + 16]
    oc_ref, stage_ref = refs[n_x + 16:]
    sub_rows = x_refs[0].shape[0] // n_sub
    n_stage = 0
    for sub in range(n_sub):
        rows = slice(sub * sub_rows, (sub + 1) * sub_rows)
        h = _rmsnorm(_read_x(x_refs, split, rows), g_ref[...]).astype(BF16)

        def project(col0, width, store, rope_base):
            y = _dot(h, w_ref[:, col0:col0 + width])
            for t in range(width // LANES):
                y_t = y[:, t * LANES:(t + 1) * LANES]
                if rope_base is not None:
                    y_t = _rope_tile(y_t, rope_ref[rope_base, rows, :], rope_ref[rope_base + 1, rows, :],
                                     rope_ref[rope_base + 2, rows, :])
                store(t, y_t)

        def slab_store(ref):
            def store(t, val):
                ref[t, rows, :] = val.astype(BF16)
            return store

        def flat_store(ref):
            def store(t, val):
                ref[rows, t * LANES:(t + 1) * LANES] = val.astype(BF16)
            return store

        def class_store(group_refs):
            def store(t, val):
                nonlocal n_stage
                (_, dil), ref = B_GROUPS[t // G_SLABS], group_refs[t // G_SLABS]
                n = sub_rows // dil
                dst = slice(sub * n, (sub + 1) * n)
                if dil == 1:
                    ref[t % G_SLABS, 0, dst, :] = val.astype(BF16)
                    return
                slot = stage_ref.at[n_stage % stage_ref.shape[0]]
                slot[...] = val
                if dil == CLASS_STRIDE:
                    n_stage += 1
                    for c in range(dil):
                        ref[t % G_SLABS, c, dst, :] = slot[pl.ds(c, n, stride=CLASS_STRIDE), :].astype(BF16)
                    return
                assert dil == CLASS_STRIDE * CLASS_STRIDE
                quarter = sub_rows // CLASS_STRIDE
                slot2 = stage_ref.at[(n_stage + 1) % stage_ref.shape[0]]
                n_stage += 2
                for c0 in range(CLASS_STRIDE):
                    slot2[c0 * quarter:(c0 + 1) * quarter, :] = slot[pl.ds(c0, quarter, stride=CLASS_STRIDE), :]
                for c0 in range(CLASS_STRIDE):
                    for c1 in range(CLASS_STRIDE):
                        ref[t % G_SLABS, CLASS_STRIDE * c1 + c0, dst, :] = (
                            slot2[pl.ds(c0 * quarter + c1, n, stride=CLASS_STRIDE), :].astype(BF16))
            return store

        rope_q, rope_k = 0, 3
        col = 0
        project(col, A_Q_W, slab_store(qa_ref), rope_q); col += A_Q_W
        project(col, A_KV_W, flat_store(ka_ref), rope_k); col += A_KV_W
        project(col, A_KV_W, flat_store(va_ref), None); col += A_KV_W
        project(col, B_W, class_store(b_refs[0:3]), rope_q); col += B_W
        project(col, B_W, class_store(b_refs[3:6]), rope_k); col += B_W
        project(col, B_W, class_store(b_refs[6:9]), None); col += B_W

        qc = (_dot(h, w_ref[:, col:col + C_Q_W]) * (C_HEAD_DIM ** -0.5 * LOG2E)).astype(BF16)
        for hd in range(C_HEADS):
            lo, hi = hd * C_HEAD_DIM, (hd + 1) * C_HEAD_DIM
            s = _dot_nt(qc[:, lo:hi], mkv_ref[:, lo:hi])
            m = jnp.max(s, axis=-1, keepdims=True)
            p = jnp.exp2(s - m)
            denom = jnp.sum(p, axis=-1, keepdims=True)
            o = _dot(p.astype(BF16), mkv_ref[:, C_Q_W + lo:C_Q_W + hi])
            oc_ref[rows, lo:hi] = (o * (1.0 / denom)).astype(BF16)


def _proj_call(xs, g_pre, w_qkv, rope_tabs, mkv, layer):
    b = sum(x.shape[0] for x in xs)
    s = xs[0].shape[1]
    tm = TOKEN_TILE
    tok = lambda w: pl.BlockSpec((None, tm, w), lambda i, j: (i, j, 0))
    slab = lambda n: pl.BlockSpec((None, n, tm, LANES), lambda i, j: (i, 0, j, 0))
    slab_shape = lambda n: jax.ShapeDtypeStruct((b, n, s, LANES), BF16)
    flat_shape = lambda w: jax.ShapeDtypeStruct((b, s, w), BF16)
    group_specs = [_class_spec(dil, tm) for _ in range(3) for _, dil in B_GROUPS]
    group_shapes = [_class_shape(b, s, dil, BF16) for _ in range(3) for _, dil in B_GROUPS]
    outs = pl.pallas_call(
        functools.partial(_proj_kernel, n_x=len(xs), split=xs[0].shape[0], n_sub=PROJ_SUBTILES),
        grid=(b, s // tm),
        in_specs=_x_specs(xs, tm) + [
            _const_spec((1, D_MODEL)),
            _const_spec((D_MODEL, QKV_W)),
            pl.BlockSpec((6, tm, LANES), lambda i, j: (0, j, 0)),
            pl.BlockSpec((None, None, N_MEM, 2 * C_Q_W), lambda i, j: (layer, i, 0, 0)),
        ],
        out_specs=[slab(A_SLABS), tok(A_KV_W), tok(A_KV_W)] + group_specs + [tok(C_Q_W)],
        out_shape=[slab_shape(A_SLABS), flat_shape(A_KV_W), flat_shape(A_KV_W)] + group_shapes
                  + [flat_shape(C_Q_W)],
        scratch_shapes=[pltpu.VMEM((PROJ_STAGE_SLOTS, tm // PROJ_SUBTILES, LANES), F32)],
        compiler_params=_params(2),
        name="proj",
    )(*xs, g_pre, w_qkv, rope_tabs, mkv)
    qa, ka, va = outs[:3]
    return qa, ka, va, outs[3:6], outs[6:9], outs[9:12], outs[12]


def _head_lane_masks():
    lane = lax.broadcasted_iota(jnp.int32, (1, LANES), 1)
    lo = (lane < HEAD_DIM).astype(BF16)
    return lo, (1 - lo).astype(BF16)


def _band_bias(tq, kw, q0_minus_k0, radius):
    d = (lax.broadcasted_iota(jnp.int32, (tq, kw), 0) + q0_minus_k0
         - lax.broadcasted_iota(jnp.int32, (tq, kw), 1))
    return jnp.where(jnp.abs(d) <= radius, 0.0, NEG_INF).astype(F32)


def _attend(q_t, k_pair, v_pair, bias, lane_is_lo, sinks):
    kw = k_pair.shape[0] // 2
    s = _dot_nt(q_t, k_pair)
    ms, ps = [], []
    for half in range(2):
        s_h = (s[:, half * kw:(half + 1) * kw] + bias).astype(BF16)
        m_h = jnp.broadcast_to(jnp.max(s_h, axis=-1, keepdims=True), (s_h.shape[0], LANES))
        if sinks is not None:
            m_h = jnp.maximum(m_h, sinks[half].astype(BF16))
        ps.append(jnp.exp2(s_h - jnp.concatenate([m_h] * (kw // LANES), axis=1)))
        ms.append(m_h.astype(F32))
    m = jnp.where(lane_is_lo, ms[0], ms[1])
    r = _dot(jnp.concatenate(ps, axis=1), v_pair)
    denom = r[:, LANES:]
    if sinks is not None:
        denom = denom + jnp.exp2(jnp.where(lane_is_lo, sinks[0], sinks[1]) - m)
    return r[:, :LANES] * (1.0 / denom), m, denom


def _fill_bias_table(bias_ref, tq, kw, radius):
    for n in range(bias_ref.shape[0]):
        bias_ref[n] = _band_bias(tq, kw, n * radius, radius)


def _pair_kv(k, v, lo, hi):
    k_pair = jnp.concatenate([k * lo, k * hi], axis=0)
    v_pair = jnp.concatenate([jnp.concatenate([v * lo, jnp.broadcast_to(lo, v.shape)], axis=1),
                              jnp.concatenate([v * hi, jnp.broadcast_to(hi, v.shape)], axis=1)], axis=0)
    return k_pair, v_pair


def _attend_rows(q_t, k, v, bias, lo, hi, lane_is_lo):
    tq, kw = q_t.shape[0], k.shape[0]
    s = _dot_nt(jnp.concatenate([q_t * lo, q_t * hi], axis=0), k)
    s = (s.reshape(2, tq, kw) + bias[None]).reshape(2 * tq, kw).astype(BF16)
    m = jnp.broadcast_to(jnp.max(s, axis=-1, keepdims=True), (2 * tq, LANES))
    p = jnp.exp2(s - jnp.concatenate([m] * (kw // LANES), axis=1))
    r = _dot(p, jnp.concatenate([v, jnp.ones(v.shape, BF16)], axis=1))
    o = jnp.where(lane_is_lo, r[:tq, :LANES], r[tq:, :LANES])
    denom = jnp.where(lane_is_lo, r[:tq, LANES:], r[tq:, LANES:])
    m = m.astype(F32)
    return o * (1.0 / denom), jnp.where(lane_is_lo, m[:tq], m[tq:]), denom


def _attn_a_kernel(sink_ref, q_ref, k_ref, v_ref, o_ref, bias_ref, *, tq, kw, seq):
    lo, hi = _head_lane_masks()
    lane_is_lo = lax.broadcasted_iota(jnp.int32, (1, LANES), 1) < HEAD_DIM
    sinks = [jnp.full((tq, LANES), sink_ref[h] * LOG2E, F32) for h in range(A_HEADS)]
    _fill_bias_table(bias_ref, tq, kw, A_RADIUS)

    def block(i, carry):
        q0 = pl.multiple_of(i * tq, tq)
        start = pl.multiple_of(jnp.clip(q0 - A_RADIUS, 0, seq - kw), LANES)
        bias = bias_ref[(q0 - start) // A_RADIUS]
        k_pair, v_pair = _pair_kv(k_ref[pl.ds(start, kw), :], v_ref[pl.ds(start, kw), :], lo, hi)
        for t in range(A_SLABS):
            q_t = q_ref[t, pl.ds(q0, tq), :]
            o, _, _ = _attend(q_t, k_pair, v_pair, bias, lane_is_lo,
                              (sinks[A_HEAD_ORDER[2 * t]], sinks[A_HEAD_ORDER[2 * t + 1]]))
            o_ref[t, pl.ds(q0, tq), :] = o.astype(BF16)
        return carry

    lax.fori_loop(0, seq // tq, block, 0, unroll=ATTN_UNROLL)


def _attn_b_kernel(q_ref, k_ref, v_ref, o_ref, lse_ref, bias_ref, *, dil, length, radius, tq, kw):
    lo, hi = _head_lane_masks()
    lane_is_lo = lax.broadcasted_iota(jnp.int32, (1, LANES), 1) < HEAD_DIM
    nblk = length // tq
    _fill_bias_table(bias_ref, tq, kw, radius)

    def unit(u, carry):
        c = u // nblk if nblk > 1 else u
        i = u - c * nblk if nblk > 1 else 0
        q0 = pl.multiple_of(i * tq, tq)
        start = pl.multiple_of(jnp.clip(q0 - radius, 0, length - kw), radius)
        bias = bias_ref[(q0 - start) // radius] if nblk > 1 else bias_ref[0]
        q_rows, win = pl.ds(q0, tq), pl.ds(start, kw)
        for t in range(G_SLABS):
            q_t, k, v = q_ref[t, c, q_rows, :], k_ref[t, c, win, :], v_ref[t, c, win, :]
            if kw >= 2 * LANES:
                o, m, denom = _attend_rows(q_t, k, v, bias, lo, hi, lane_is_lo)
            else:
                o, m, denom = _attend(q_t, *_pair_kv(k, v, lo, hi), bias, lane_is_lo, None)
            o_ref[t, c, q_rows, :] = o.astype(BF16)
            lse_ref[t, c, q_rows, :] = m + jnp.log2(denom)
        return carry

    lax.fori_loop(0, dil * nblk, unit, 0, unroll=ATTN_UNROLL)


def _b_geometry(seq, group):
    window, dil = B_GROUPS[group]
    length = seq // dil
    radius = window // (2 * dil)
    tq = min(ATTN_TQ, length)
    return dict(dil=dil, length=length, radius=radius, tq=tq, kw=min(tq + 2 * radius, length))


def _attn_kernel(sink_ref, qa_ref, ka_ref, va_ref, *refs, seq):
    n = len(B_GROUPS)
    qkv_refs, refs = refs[:3 * n], refs[3 * n:]
    oa_ref, out_refs, bias_refs = refs[0], refs[1:1 + 2 * n], refs[1 + 2 * n:]
    _attn_a_kernel(sink_ref, qa_ref, ka_ref, va_ref, oa_ref, bias_refs[0],
                   tq=ATTN_TQ, kw=ATTN_TQ + 2 * A_RADIUS, seq=seq)
    for g in range(n):
        _attn_b_kernel(*qkv_refs[3 * g:3 * g + 3], *out_refs[2 * g:2 * g + 2], bias_refs[1 + g],
                       **_b_geometry(seq, g))


def _attn_call(sink, qa, ka, va, qb, kb, vb):
    b, _, s, _ = qa.shape
    whole = lambda shape: pl.BlockSpec((None,) + tuple(shape[1:]), lambda i: (i,) + (0,) * (len(shape) - 1))
    geo = [_b_geometry(s, g) for g in range(len(B_GROUPS))]
    qkv = [a for g in range(len(B_GROUPS)) for a in (qb[g], kb[g], vb[g])]
    out_shape = [jax.ShapeDtypeStruct(qa.shape, BF16)]
    for g in geo:
        out_shape += [_class_shape(b, s, g["dil"], BF16), _class_shape(b, s, g["dil"], F32)]
    bias = [pltpu.VMEM((3, ATTN_TQ, ATTN_TQ + 2 * A_RADIUS), F32)]
    bias += [pltpu.VMEM((3 if g["length"] > g["tq"] else 1, g["tq"], g["kw"]), F32) for g in geo]
    outs = pl.pallas_call(
        functools.partial(_attn_kernel, seq=s),
        grid=(b,),
        in_specs=[pl.BlockSpec(memory_space=pltpu.SMEM)] + [whole(a.shape) for a in (qa, ka, va, *qkv)],
        out_specs=[whole(o.shape) for o in out_shape],
        out_shape=out_shape,
        scratch_shapes=bias,
        compiler_params=_params(1),
        name="attn",
    )(sink, qa, ka, va, *qkv)
    return outs[0], outs[1::2], outs[2::2]


def _lanes(ref, rows):
    return jnp.concatenate([ref[t, rows, :] for t in range(ref.shape[0])], axis=-1)


def _merge_kernel(*refs, n_x, split, n_sub):
    x_refs = refs[:n_x]
    (oa_ref, ob0_ref, ob1_ref, ob2_ref, l0_ref, l1_ref, l2_ref, oc_ref,
     g_pre_ref, g_post_ref, w_gate_ref, w_oa_ref, w_ob_ref, w_oc_ref, w_out_ref, out_ref, stage_ref) = refs[n_x:]
    sub_rows = out_ref.shape[0] // n_sub
    n_stage = 0

    def natural(ref, sub):
        nonlocal n_stage
        dil = ref.shape[1]
        n = sub_rows // dil
        src = slice(sub * n, (sub + 1) * n)
        tiles = []
        for t in range(G_SLABS):
            if dil == 1:
                tiles.append(ref[t, 0, src, :].astype(F32))
                continue
            slot = stage_ref.at[n_stage % stage_ref.shape[0]]
            if dil == CLASS_STRIDE:
                n_stage += 1
                for c in range(dil):
                    slot[pl.ds(c, n, stride=CLASS_STRIDE), :] = ref[t, c, src, :].astype(F32)
                tiles.append(slot[...])
                continue
            assert dil == CLASS_STRIDE * CLASS_STRIDE
            quarter = sub_rows // CLASS_STRIDE
            slot2 = stage_ref.at[(n_stage + 1) % stage_ref.shape[0]]
            n_stage += 2
            for c0 in range(CLASS_STRIDE):
                for c1 in range(CLASS_STRIDE):
                    slot[pl.ds(c0 * quarter + c1, n, stride=CLASS_STRIDE), :] = (
                        ref[t, CLASS_STRIDE * c1 + c0, src, :].astype(F32))
            for c0 in range(CLASS_STRIDE):
                slot2[pl.ds(c0, quarter, stride=CLASS_STRIDE), :] = slot[c0 * quarter:(c0 + 1) * quarter, :]
            tiles.append(slot2[...])
        return jnp.concatenate(tiles, axis=-1)

    for sub in range(n_sub):
        rows = slice(sub * sub_rows, (sub + 1) * sub_rows)
        x = _read_x(x_refs, split, rows)
        h = _rmsnorm(x, g_pre_ref[...]).astype(BF16)

        l0, l1, l2 = natural(l0_ref, sub), natural(l1_ref, sub), natural(l2_ref, sub)
        m = jnp.maximum(jnp.maximum(l0, l1), l2)
        e0, e1, e2 = jnp.exp2(l0 - m), jnp.exp2(l1 - m), jnp.exp2(l2 - m)
        num = e0 * natural(ob0_ref, sub) + e1 * natural(ob1_ref, sub) + e2 * natural(ob2_ref, sub)
        ob = (num * (1.0 / (e0 + e1 + e2))).astype(BF16)

        merged = None
        for br, (o_br, w_ref) in enumerate(((_lanes(oa_ref, rows), w_oa_ref), (ob, w_ob_ref),
                                            (oc_ref[rows, :], w_oc_ref))):
            gate = jax.nn.sigmoid(_dot(h, w_gate_ref[:, br * D_MODEL:(br + 1) * D_MODEL]))
            term = gate * _dot(o_br, w_ref[...])
            merged = term if merged is None else merged + term
        z = _dot(merged.astype(BF16), w_out_ref[...])
        out_ref[rows, :] = x + _rmsnorm(z, g_post_ref[...])


def _merge_call(xs, oa, obs, lses, oc, g_pre, g_post, w_gate, w_oa, w_ob, w_oc, w_out):
    b, _, s, _ = oa.shape
    tm = TOKEN_TILE
    tok = lambda w: pl.BlockSpec((None, tm, w), lambda i, j: (i, j, 0))
    slab = lambda n: pl.BlockSpec((None, n, tm, LANES), lambda i, j: (i, 0, j, 0))
    groups = [_class_spec(dil, tm) for _ in range(2) for _, dil in B_GROUPS]
    return pl.pallas_call(
        functools.partial(_merge_kernel, n_x=len(xs), split=xs[0].shape[0], n_sub=MERGE_SUBTILES),
        grid=(b, s // tm),
        in_specs=_x_specs(xs, tm) + [slab(A_SLABS)] + groups + [
            tok(C_Q_W),
            _const_spec((1, D_MODEL)), _const_spec((1, D_MODEL)),
            _const_spec((D_MODEL, 3 * D_MODEL)),
            _const_spec((A_Q_W, D_MODEL)), _const_spec((B_GW, D_MODEL)), _const_spec((C_Q_W, D_MODEL)),
            _const_spec((D_MODEL, D_MODEL))],
        out_specs=tok(D_MODEL),
        out_shape=jax.ShapeDtypeStruct((b, s, D_MODEL), F32),
        scratch_shapes=[pltpu.VMEM((MERGE_STAGE_SLOTS, tm // MERGE_SUBTILES, LANES), F32)],
        compiler_params=_params(2),
        name="merge",
    )(*xs, oa, *obs, *lses, oc, g_pre, g_post, w_gate, w_oa, w_ob, w_oc, w_out)


def _ffn_kernel(x_ref, g_pre_ref, g_post_ref, w_in_ref, w_out_ref, out_ref, act_ref, *, n_sub):
    sub_rows = x_ref.shape[0] // n_sub
    for sub in range(n_sub):
        rows = slice(sub * sub_rows, (sub + 1) * sub_rows)
        x = x_ref[rows, :]
        h = _rmsnorm(x, g_pre_ref[...]).astype(BF16)
        for c0 in range(0, FFN_HIDDEN, FFN_CHUNK):
            g = _dot(h, w_in_ref[:, c0:c0 + FFN_CHUNK])
            u = _dot(h, w_in_ref[:, FFN_HIDDEN + c0:FFN_HIDDEN + c0 + FFN_CHUNK])
            act_ref[rows, c0:c0 + FFN_CHUNK] = (g * jax.nn.sigmoid(g) * u).astype(BF16)
        y = _dot(act_ref[rows, :], w_out_ref[...])
        out_ref[rows, :] = x + _rmsnorm(y, g_post_ref[...])


def _ffn_call(x, g_pre, g_post, w_in, w_out, batch_offset, batch):
    s = x.shape[1]
    tm = TOKEN_TILE
    return pl.pallas_call(
        functools.partial(_ffn_kernel, n_sub=FFN_SUBTILES),
        grid=(batch, s // tm),
        in_specs=[pl.BlockSpec((None, tm, D_MODEL), lambda i, j: (i + batch_offset, j, 0)),
                  _const_spec((1, D_MODEL)), _const_spec((1, D_MODEL)),
                  _const_spec((D_MODEL, 2 * FFN_HIDDEN)), _const_spec((FFN_HIDDEN, D_MODEL))],
        out_specs=pl.BlockSpec((None, tm, D_MODEL), lambda i, j: (i, j, 0)),
        out_shape=jax.ShapeDtypeStruct((batch, s, D_MODEL), F32),
        scratch_shapes=[pltpu.VMEM((tm, FFN_HIDDEN), BF16)],
        compiler_params=_params(2),
        name="ffn",
    )(x, g_pre, g_post, w_in, w_out)


def _rope_tables(seq):
    inv_freq = ROPE_THETA ** (-jnp.arange(0, ROPE_DIMS, 2, dtype=F32) / ROPE_DIMS)
    ang = jnp.arange(seq, dtype=F32)[:, None] * inv_freq[None, :]
    cos, sin = jnp.cos(ang), jnp.sin(ang)
    pad = jnp.zeros((seq, HEAD_DIM - ROPE_DIMS), F32)
    zero = jnp.zeros_like(sin)
    c = jnp.concatenate([cos, cos, pad + 1.0], axis=-1)
    s_up = jnp.concatenate([-sin, zero, pad], axis=-1)
    s_dn = jnp.concatenate([zero, sin, pad], axis=-1)
    tabs = jnp.stack([jnp.tile(t, (1, LANES // HEAD_DIM)) for t in (c, s_up, s_dn)])
    return jnp.concatenate([tabs * (HEAD_DIM ** -0.5 * LOG2E), tabs], axis=0)


def _layer_weights(l, norm_mix_pre, norm_mix_post, w_in, sink_a, w_o_a, w_o_b, w_o_c, w_out,
                   norm_ffn_pre, norm_ffn_post, w_ffn_in, w_ffn_out):
    order = jnp.array(A_HEAD_ORDER)
    w_in_l = w_in[l]
    w_qa = w_in_l[:, :A_Q_W].reshape(D_MODEL, A_HEADS, HEAD_DIM)[:, order].reshape(D_MODEL, A_Q_W)
    w_qkv = jnp.concatenate([w_qa, w_in_l[:, A_Q_W:QKV_W]], axis=1).astype(BF16)
    w_oa = w_o_a[l].reshape(A_HEADS, HEAD_DIM, D_MODEL)[order].reshape(A_Q_W, D_MODEL).astype(BF16)
    row = lambda g: g[l].reshape(1, D_MODEL)
    return dict(
        g_mix_pre=row(norm_mix_pre), g_mix_post=row(norm_mix_post), w_qkv=w_qkv,
        w_gate=w_in_l[:, QKV_W:].astype(BF16), sink=sink_a[l],
        w_oa=w_oa, w_ob=w_o_b[l].astype(BF16), w_oc=w_o_c[l].astype(BF16), w_out=w_out[l].astype(BF16),
        g_ffn_pre=row(norm_ffn_pre), g_ffn_post=row(norm_ffn_post),
        w_ffn_in=w_ffn_in[l].astype(BF16), w_ffn_out=w_ffn_out[l].astype(BF16))


@jax.jit
def kernel(x_prompt, x_sample, mem_prompt, mem_sample, norm_mix_pre, norm_mix_post, norm_mem, w_in, sink_a,
           w_mem_kv, w_o_a, w_o_b, w_o_c, w_out, norm_ffn_pre, norm_ffn_post, w_ffn_in, w_ffn_out):
    layers = [_layer_weights(l, norm_mix_pre, norm_mix_post, w_in, sink_a, w_o_a, w_o_b, w_o_c, w_out,
                             norm_ffn_pre, norm_ffn_post, w_ffn_in, w_ffn_out) for l in range(DEPTH)]
    rope_tabs = _rope_tables(x_prompt.shape[1])
    mkv = _memkv_call(jnp.concatenate([mem_prompt, mem_sample], axis=0),
                      norm_mem.reshape(DEPTH, 1, D_MODEL), w_mem_kv.astype(BF16))
    batches = (x_prompt.shape[0], x_sample.shape[0])
    xs = (x_prompt, x_sample)
    for l, w in enumerate(layers):
        qa, ka, va, qb, kb, vb, oc = _proj_call(xs, w["g_mix_pre"], w["w_qkv"], rope_tabs, mkv, l)
        oa, obs, lses = _attn_call(w["sink"], qa, ka, va, qb, kb, vb)
        x = _merge_call(xs, oa, obs, lses, oc, w["g_mix_pre"], w["g_mix_post"], w["w_gate"],
                        w["w_oa"], w["w_ob"], w["w_oc"], w["w_out"])
        ffn = functools.partial(_ffn_call, x, w["g_ffn_pre"], w["g_ffn_post"], w["w_ffn_in"], w["w_ffn_out"])
        if l + 1 < DEPTH:
            xs = (ffn(0, sum(batches)),)
    return (ffn(0, batches[0]), ffn(batches[0], batches[1]))
```

```python
import functools
import math

import jax
import jax.numpy as jnp
from jax import lax
from jax.experimental import pallas as pl
from jax.experimental.pallas import tpu as pltpu

D_MODEL = 1024
DEPTH = 4
HEAD_DIM = 64
A_HEADS = 8
A_KV_HEADS = 2
A_RADIUS = 128
B_GROUPS = ((128, 1), (512, 4), (2048, 16))
B_HPG = 4
C_HEADS = 4
C_HEAD_DIM = 128
N_MEM = 256
ROPE_THETA = 500000.0
ROPE_DIMS = HEAD_DIM // 4
ROPE_HALF = ROPE_DIMS // 2
FFN_HIDDEN = 2816
A_Q_W = A_HEADS * HEAD_DIM
A_KV_W = A_KV_HEADS * HEAD_DIM
B_GW = B_HPG * HEAD_DIM
B_W = B_GW * len(B_GROUPS)
C_Q_W = C_HEADS * C_HEAD_DIM
QKV_W = A_Q_W + 2 * A_KV_W + 3 * B_W + C_Q_W
EPS = 1e-6
NEG_INF = -1e30
LOG2E = math.log2(math.e)

LANES = 128
VMEM_LIMIT = 56 * 1024 * 1024
A_SLABS = A_Q_W // LANES
G_SLABS = B_GW // LANES
CLASS_STRIDE = 4

TOKEN_TILE = 1024
PROJ_SUBTILES = 1
PROJ_STAGE_SLOTS = 6
MERGE_SUBTILES = 4
MERGE_STAGE_SLOTS = 12
FFN_SUBTILES = 4
FFN_CHUNK = 256
FUSED_TILE = 512
FUSED_SUBTILES = 2
MEMKV_BATCH = 8
ATTN_TQ = 128

BF16 = jnp.bfloat16
F32 = jnp.float32

A_HEAD_ORDER = (0, 4, 1, 5, 2, 6, 3, 7)


def _const_spec(shape):
    nd = len(shape)
    return pl.BlockSpec(shape, lambda *_: (0,) * nd, pipeline_mode=pl.Buffered(1))


def _params(n_grid):
    return pltpu.CompilerParams(dimension_semantics=("arbitrary",) * n_grid,
                                vmem_limit_bytes=VMEM_LIMIT)


def _rmsnorm(x, g):
    return x * lax.rsqrt(jnp.mean(x * x, axis=-1, keepdims=True) + EPS) * g


def _dot(a, b):
    return jnp.dot(a, b, preferred_element_type=F32)


def _dot_nt(a, b):
    return lax.dot_general(a, b, (((1,), (1,)), ((), ())), preferred_element_type=F32)


def _x_specs(xs, tm):
    if len(xs) == 1:
        return [pl.BlockSpec((None, tm, D_MODEL), lambda i, j: (i, j, 0))]
    b0 = xs[0].shape[0]
    last_j = xs[0].shape[1] // tm - 1
    first = pl.BlockSpec((None, tm, D_MODEL),
                         lambda i, j: (jnp.minimum(i, b0 - 1), jnp.where(i < b0, j, last_j), 0))
    second = pl.BlockSpec((None, tm, D_MODEL),
                          lambda i, j: (jnp.maximum(i - b0, 0), jnp.where(i < b0, 0, j), 0))
    return [first, second]


def _class_shape(b, s, dil, dtype):
    return jax.ShapeDtypeStruct((b, G_SLABS, dil, s // dil, LANES), dtype)


def _class_spec(dil, tm):
    return pl.BlockSpec((None, G_SLABS, dil, tm // dil, LANES), lambda i, j: (i, 0, 0, j, 0))


def _read_x(x_refs, split, rows):
    if len(x_refs) == 1:
        return x_refs[0][rows, :]
    return jnp.where(pl.program_id(0) < split, x_refs[0][rows, :], x_refs[1][rows, :])


def _memkv_kernel(mem_ref, g_ref, w_ref, o_ref):
    nb = mem_ref.shape[0]
    h = _rmsnorm(mem_ref[...].reshape(nb * N_MEM, D_MODEL), g_ref[...]).astype(BF16)
    o_ref[...] = _dot(h, w_ref[...]).astype(BF16).reshape(nb, N_MEM, 2 * C_Q_W)


def _memkv_call(mem, g_mem, w_mem_kv):
    b = mem.shape[0]
    nb = math.gcd(b, MEMKV_BATCH)
    return pl.pallas_call(
        _memkv_kernel,
        grid=(DEPTH, b // nb),
        in_specs=[
            pl.BlockSpec((nb, N_MEM, D_MODEL), lambda l, i: (i, 0, 0)),
            pl.BlockSpec((None, 1, D_MODEL), lambda l, i: (l, 0, 0)),
            pl.BlockSpec((None, D_MODEL, 2 * C_Q_W), lambda l, i: (l, 0, 0)),
        ],
        out_specs=pl.BlockSpec((None, nb, N_MEM, 2 * C_Q_W), lambda l, i: (l, i, 0, 0)),
        out_shape=jax.ShapeDtypeStruct((DEPTH, b, N_MEM, 2 * C_Q_W), BF16),
        compiler_params=_params(2),
        name="memkv",
    )(mem, g_mem, w_mem_kv)


def _rope_tile(y, c, s_up, s_dn):
    return (y * c + pltpu.roll(y, LANES - ROPE_HALF, 1) * s_up + pltpu.roll(y, ROPE_HALF, 1) * s_dn)


def _proj_kernel(*refs, n_x, split, n_sub):
    x_refs = refs[:n_x]
    (g_ref, w_ref, rope_ref, mkv_ref, qa_ref, ka_ref, va_ref) = refs[n_x:n_x + 7]
    b_refs = refs[n_x + 7:n_x + 16]
    oc_ref, stage_ref = refs[n_x + 16:]
    sub_rows = x_refs[0].shape[0] // n_sub
    n_stage = 0
    for sub in range(n_sub):
        rows = slice(sub * sub_rows, (sub + 1) * sub_rows)
        h = _rmsnorm(_read_x(x_refs, split, rows), g_ref[...]).astype(BF16)

        def project(col0, width, store, rope_base):
            y = _dot(h, w_ref[:, col0:col0 + width])
            for t in range(width // LANES):
                y_t = y[:, t * LANES:(t + 1) * LANES]
                if rope_base is not None:
                    y_t = _rope_tile(y_t, rope_ref[rope_base, rows, :], rope_ref[rope_base + 1, rows, :],
                                     rope_ref[rope_base + 2, rows, :])
                store(t, y_t)

        def slab_store(ref):
            def store(t, val):
                ref[t, rows, :] = val.astype(BF16)
            return store

        def flat_store(ref):
            def store(t, val):
                ref[rows, t * LANES:(t + 1) * LANES] = val.astype(BF16)
            return store

        def class_store(group_refs):
            def store(t, val):
                nonlocal n_stage
                (_, dil), ref = B_GROUPS[t // G_SLABS], group_refs[t // G_SLABS]
                n = sub_rows // dil
                dst = slice(sub * n, (sub + 1) * n)
                if dil == 1:
                    ref[t % G_SLABS, 0, dst, :] = val.astype(BF16)
                    return
                slot = stage_ref.at[n_stage % stage_ref.shape[0]]
                slot[...] = val
                if dil == CLASS_STRIDE:
                    n_stage += 1
                    for c in range(dil):
                        ref[t % G_SLABS, c, dst, :] = slot[pl.ds(c, n, stride=CLASS_STRIDE), :].astype(BF16)
                    return
                assert dil == CLASS_STRIDE * CLASS_STRIDE
                quarter = sub_rows // CLASS_STRIDE
                slot2 = stage_ref.at[(n_stage + 1) % stage_ref.shape[0]]
                n_stage += 2
                for c0 in range(CLASS_STRIDE):
                    slot2[c0 * quarter:(c0 + 1) * quarter, :] = slot[pl.ds(c0, quarter, stride=CLASS_STRIDE), :]
                for c0 in range(CLASS_STRIDE):
                    for c1 in range(CLASS_STRIDE):
                        ref[t % G_SLABS, CLASS_STRIDE * c1 + c0, dst, :] = (
                            slot2[pl.ds(c0 * quarter + c1, n, stride=CLASS_STRIDE), :].astype(BF16))
            return store

        rope_q, rope_k = 0, 3
        col = 0
        project(col, A_Q_W, slab_store(qa_ref), rope_q); col += A_Q_W
        project(col, A_KV_W, flat_store(ka_ref), rope_k); col += A_KV_W
        project(col, A_KV_W, flat_store(va_ref), None); col += A_KV_W
        project(col, B_W, class_store(b_refs[0:3]), rope_q); col += B_W
        project(col, B_W, class_store(b_refs[3:6]), rope_k); col += B_W
        project(col, B_W, class_store(b_refs[6:9]), None); col += B_W

        qc = (_dot(h, w_ref[:, col:col + C_Q_W]) * (C_HEAD_DIM ** -0.5 * LOG2E)).astype(BF16)
        for hd in range(C_HEADS):
            lo, hi = hd * C_HEAD_DIM, (hd + 1) * C_HEAD_DIM
            s = _dot_nt(qc[:, lo:hi], mkv_ref[:, lo:hi])
            m = jnp.max(s, axis=-1, keepdims=True)
            p = jnp.exp2(s - m)
            denom = jnp.sum(p, axis=-1, keepdims=True)
            o = _dot(p.astype(BF16), mkv_ref[:, C_Q_W + lo:C_Q_W + hi])
            oc_ref[rows, lo:hi] = (o * (1.0 / denom)).astype(BF16)


def _proj_call(xs, g_pre, w_qkv, rope_tabs, mkv, layer):
    b = sum(x.shape[0] for x in xs)
    s = xs[0].shape[1]
    tm = TOKEN_TILE
    tok = lambda w: pl.BlockSpec((None, tm, w), lambda i, j: (i, j, 0))
    slab = lambda n: pl.BlockSpec((None, n, tm, LANES), lambda i, j: (i, 0, j, 0))
    slab_shape = lambda n: jax.ShapeDtypeStruct((b, n, s, LANES), BF16)
    flat_shape = lambda w: jax.ShapeDtypeStruct((b, s, w), BF16)
    group_specs = [_class_spec(dil, tm) for _ in range(3) for _, dil in B_GROUPS]
    group_shapes = [_class_shape(b, s, dil, BF16) for _ in range(3) for _, dil in B_GROUPS]
    outs = pl.pallas_call(
        functools.partial(_proj_kernel, n_x=len(xs), split=xs[0].shape[0], n_sub=PROJ_SUBTILES),
        grid=(b, s // tm),
        in_specs=_x_specs(xs, tm) + [
            _const_spec((1, D_MODEL)),
            _const_spec((D_MODEL, QKV_W)),
            pl.BlockSpec((6, tm, LANES), lambda i, j: (0, j, 0)),
            pl.BlockSpec((None, None, N_MEM, 2 * C_Q_W), lambda i, j: (layer, i, 0, 0)),
        ],
        out_specs=[slab(A_SLABS), tok(A_KV_W), tok(A_KV_W)] + group_specs + [tok(C_Q_W)],
        out_shape=[slab_shape(A_SLABS), flat_shape(A_KV_W), flat_shape(A_KV_W)] + group_shapes
                  + [flat_shape(C_Q_W)],
        scratch_shapes=[pltpu.VMEM((PROJ_STAGE_SLOTS, tm // PROJ_SUBTILES, LANES), F32)],
        compiler_params=_params(2),
        name="proj",
    )(*xs, g_pre, w_qkv, rope_tabs, mkv)
    qa, ka, va = outs[:3]
    return qa, ka, va, outs[3:6], outs[6:9], outs[9:12], outs[12]


def _head_lane_masks():
    lane = lax.broadcasted_iota(jnp.int32, (1, LANES), 1)
    lo = (lane < HEAD_DIM).astype(BF16)
    return lo, (1 - lo).astype(BF16)


def _band_bias(tq, kw, q0_minus_k0, radius):
    d = (lax.broadcasted_iota(jnp.int32, (tq, kw), 0) + q0_minus_k0
         - lax.broadcasted_iota(jnp.int32, (tq, kw), 1))
    return jnp.where(jnp.abs(d) <= radius, 0.0, NEG_INF).astype(F32)


def _attend(q_t, k_pair, v_pair, bias, lane_is_lo, sinks):
    kw = k_pair.shape[0] // 2
    s = _dot_nt(q_t, k_pair)
    ms, ps = [], []
    for half in range(2):
        s_h = (s[:, half * kw:(half + 1) * kw] + bias).astype(BF16)
        m_h = jnp.broadcast_to(jnp.max(s_h, axis=-1, keepdims=True), (s_h.shape[0], LANES))
        if sinks is not None:
            m_h = jnp.maximum(m_h, sinks[half].astype(BF16))
        ps.append(jnp.exp2(s_h - jnp.concatenate([m_h] * (kw // LANES), axis=1)))
        ms.append(m_h.astype(F32))
    m = jnp.where(lane_is_lo, ms[0], ms[1])
    r = _dot(jnp.concatenate(ps, axis=1), v_pair)
    denom = r[:, LANES:]
    if sinks is not None:
        denom = denom + jnp.exp2(jnp.where(lane_is_lo, sinks[0], sinks[1]) - m)
    return r[:, :LANES] * (1.0 / denom), m, denom


def _fill_bias_table(bias_ref, tq, kw, radius):
    for n in range(bias_ref.shape[0]):
        bias_ref[n] = _band_bias(tq, kw, n * radius, radius)


def _pair_kv(k, v, lo, hi):
    k_pair = jnp.concatenate([k * lo, k * hi], axis=0)
    v_pair = jnp.concatenate([jnp.concatenate([v * lo, jnp.broadcast_to(lo, v.shape)], axis=1),
                              jnp.concatenate([v * hi, jnp.broadcast_to(hi, v.shape)], axis=1)], axis=0)
    return k_pair, v_pair


def _attend_rows(q_t, k, v, bias, lo, hi, lane_is_lo):
    tq, kw = q_t.shape[0], k.shape[0]
    s = _dot_nt(jnp.concatenate([q_t * lo, q_t * hi], axis=0), k)
    s = (s.reshape(2, tq, kw) + bias[None]).reshape(2 * tq, kw).astype(BF16)
    m = jnp.broadcast_to(jnp.max(s, axis=-1, keepdims=True), (2 * tq, LANES))
    p = jnp.exp2(s - jnp.concatenate([m] * (kw // LANES), axis=1))
    r = _dot(p, jnp.concatenate([v, jnp.ones(v.shape, BF16)], axis=1))
    o = jnp.where(lane_is_lo, r[:tq, :LANES], r[tq:, :LANES])
    denom = jnp.where(lane_is_lo, r[:tq, LANES:], r[tq:, LANES:])
    m = m.astype(F32)
    return o * (1.0 / denom), jnp.where(lane_is_lo, m[:tq], m[tq:]), denom


def _attn_a_kernel(sink_ref, q_ref, k_ref, v_ref, o_ref, bias_ref, *, tq, kw, seq):
    lo, hi = _head_lane_masks()
    lane_is_lo = lax.broadcasted_iota(jnp.int32, (1, LANES), 1) < HEAD_DIM
    sinks = [jnp.full((tq, LANES), sink_ref[h] * LOG2E, F32) for h in range(A_HEADS)]
    _fill_bias_table(bias_ref, tq, kw, A_RADIUS)

    for i in range(seq // tq):
        q0 = i * tq
        start = min(max(q0 - A_RADIUS, 0), seq - kw)
        bias = bias_ref[(q0 - start) // A_RADIUS]
        k_pair, v_pair = _pair_kv(k_ref[pl.ds(start, kw), :], v_ref[pl.ds(start, kw), :], lo, hi)
        for t in range(A_SLABS):
            q_t = q_ref[t, pl.ds(q0, tq), :]
            o, _, _ = _attend(q_t, k_pair, v_pair, bias, lane_is_lo,
                              (sinks[A_HEAD_ORDER[2 * t]], sinks[A_HEAD_ORDER[2 * t + 1]]))
            o_ref[t, pl.ds(q0, tq), :] = o.astype(BF16)


def _attn_b_kernel(q_ref, k_ref, v_ref, o_ref, lse_ref, bias_ref, *, dil, length, radius, tq, kw):
    lo, hi = _head_lane_masks()
    lane_is_lo = lax.broadcasted_iota(jnp.int32, (1, LANES), 1) < HEAD_DIM
    nblk = length // tq
    _fill_bias_table(bias_ref, tq, kw, radius)

    for u in range(dil * nblk):
        c, i = divmod(u, nblk)
        q0 = i * tq
        start = min(max(q0 - radius, 0), length - kw)
        bias = bias_ref[(q0 - start) // radius]
        q_rows, win = pl.ds(q0, tq), pl.ds(start, kw)
        for t in range(G_SLABS):
            q_t, k, v = q_ref[t, c, q_rows, :], k_ref[t, c, win, :], v_ref[t, c, win, :]
            if kw >= 2 * LANES:
                o, m, denom = _attend_rows(q_t, k, v, bias, lo, hi, lane_is_lo)
            else:
                o, m, denom = _attend(q_t, *_pair_kv(k, v, lo, hi), bias, lane_is_lo, None)
            o_ref[t, c, q_rows, :] = o.astype(BF16)
            lse_ref[t, c, q_rows, :] = m + jnp.log2(denom)


def _b_geometry(seq, group):
    window, dil = B_GROUPS[group]
    length = seq // dil
    radius = window // (2 * dil)
    tq = min(ATTN_TQ, length)
    return dict(dil=dil, length=length, radius=radius, tq=tq, kw=min(tq + 2 * radius, length))


def _attn_kernel(sink_ref, qa_ref, ka_ref, va_ref, *refs, seq):
    n = len(B_GROUPS)
    qkv_refs, refs = refs[:3 * n], refs[3 * n:]
    oa_ref, out_refs, bias_refs = refs[0], refs[1:1 + 2 * n], refs[1 + 2 * n:]
    _attn_a_kernel(sink_ref, qa_ref, ka_ref, va_ref, oa_ref, bias_refs[0],
                   tq=ATTN_TQ, kw=ATTN_TQ + 2 * A_RADIUS, seq=seq)
    for g in range(n):
        _attn_b_kernel(*qkv_refs[3 * g:3 * g + 3], *out_refs[2 * g:2 * g + 2], bias_refs[1 + g],
                       **_b_geometry(seq, g))


def _attn_call(sink, qa, ka, va, qb, kb, vb):
    b, _, s, _ = qa.shape
    whole = lambda shape: pl.BlockSpec((None,) + tuple(shape[1:]), lambda i: (i,) + (0,) * (len(shape) - 1))
    geo = [_b_geometry(s, g) for g in range(len(B_GROUPS))]
    qkv = [a for g in range(len(B_GROUPS)) for a in (qb[g], kb[g], vb[g])]
    out_shape = [jax.ShapeDtypeStruct(qa.shape, BF16)]
    for g in geo:
        out_shape += [_class_shape(b, s, g["dil"], BF16), _class_shape(b, s, g["dil"], F32)]
    bias = [pltpu.VMEM((3, ATTN_TQ, ATTN_TQ + 2 * A_RADIUS), F32)]
    bias += [pltpu.VMEM((3 if g["length"] > g["tq"] else 1, g["tq"], g["kw"]), F32) for g in geo]
    outs = pl.pallas_call(
        functools.partial(_attn_kernel, seq=s),
        grid=(b,),
        in_specs=[pl.BlockSpec(memory_space=pltpu.SMEM)] + [whole(a.shape) for a in (qa, ka, va, *qkv)],
        out_specs=[whole(o.shape) for o in out_shape],
        out_shape=out_shape,
        scratch_shapes=bias,
        compiler_params=_params(1),
        name="attn",
    )(sink, qa, ka, va, *qkv)
    return outs[0], outs[1::2], outs[2::2]


def _lanes(ref, rows):
    return jnp.concatenate([ref[t, rows, :] for t in range(ref.shape[0])], axis=-1)


def _merge_kernel(*refs, n_x, split, n_sub):
    x_refs = refs[:n_x]
    (oa_ref, ob0_ref, ob1_ref, ob2_ref, l0_ref, l1_ref, l2_ref, oc_ref,
     g_pre_ref, g_post_ref, w_gate_ref, w_oa_ref, w_ob_ref, w_oc_ref, w_out_ref, out_ref, stage_ref) = refs[n_x:]
    sub_rows = out_ref.shape[0] // n_sub
    n_stage = 0

    def natural(ref, sub):
        nonlocal n_stage
        dil = ref.shape[1]
        n = sub_rows // dil
        src = slice(sub * n, (sub + 1) * n)
        tiles = []
        for t in range(G_SLABS):
            if dil == 1:
                tiles.append(ref[t, 0, src, :].astype(F32))
                continue
            slot = stage_ref.at[n_stage % stage_ref.shape[0]]
            if dil == CLASS_STRIDE:
                n_stage += 1
                for c in range(dil):
                    slot[pl.ds(c, n, stride=CLASS_STRIDE), :] = ref[t, c, src, :].astype(F32)
                tiles.append(slot[...])
                continue
            assert dil == CLASS_STRIDE * CLASS_STRIDE
            quarter = sub_rows // CLASS_STRIDE
            slot2 = stage_ref.at[(n_stage + 1) % stage_ref.shape[0]]
            n_stage += 2
            for c0 in range(CLASS_STRIDE):
                for c1 in range(CLASS_STRIDE):
                    slot[pl.ds(c0 * quarter + c1, n, stride=CLASS_STRIDE), :] = (
                        ref[t, CLASS_STRIDE * c1 + c0, src, :].astype(F32))
            for c0 in range(CLASS_STRIDE):
                slot2[pl.ds(c0, quarter, stride=CLASS_STRIDE), :] = slot[c0 * quarter:(c0 + 1) * quarter, :]
            tiles.append(slot2[...])
        return jnp.concatenate(tiles, axis=-1)

    for sub in range(n_sub):
        rows = slice(sub * sub_rows, (sub + 1) * sub_rows)
        x = _read_x(x_refs, split, rows)
        h = _rmsnorm(x, g_pre_ref[...]).astype(BF16)

        l0, l1, l2 = natural(l0_ref, sub), natural(l1_ref, sub), natural(l2_ref, sub)
        m = jnp.maximum(jnp.maximum(l0, l1), l2)
        e0, e1, e2 = jnp.exp2(l0 - m), jnp.exp2(l1 - m), jnp.exp2(l2 - m)
        num = e0 * natural(ob0_ref, sub) + e1 * natural(ob1_ref, sub) + e2 * natural(ob2_ref, sub)
        ob = (num * (1.0 / (e0 + e1 + e2))).astype(BF16)

        merged = None
        for br, (o_br, w_ref) in enumerate(((_lanes(oa_ref, rows), w_oa_ref), (ob, w_ob_ref),
                                            (oc_ref[rows, :], w_oc_ref))):
            gate = jax.nn.sigmoid(_dot(h, w_gate_ref[:, br * D_MODEL:(br + 1) * D_MODEL]))
            term = gate * _dot(o_br, w_ref[...])
            merged = term if merged is None else merged + term
        z = _dot(merged.astype(BF16), w_out_ref[...])
        out_ref[rows, :] = x + _rmsnorm(z, g_post_ref[...])


def _merge_call(xs, oa, obs, lses, oc, g_pre, g_post, w_gate, w_oa, w_ob, w_oc, w_out):
    b, _, s, _ = oa.shape
    tm = TOKEN_TILE
    tok = lambda w: pl.BlockSpec((None, tm, w), lambda i, j: (i, j, 0))
    slab = lambda n: pl.BlockSpec((None, n, tm, LANES), lambda i, j: (i, 0, j, 0))
    groups = [_class_spec(dil, tm) for _ in range(2) for _, dil in B_GROUPS]
    return pl.pallas_call(
        functools.partial(_merge_kernel, n_x=len(xs), split=xs[0].shape[0], n_sub=MERGE_SUBTILES),
        grid=(b, s // tm),
        in_specs=_x_specs(xs, tm) + [slab(A_SLABS)] + groups + [
            tok(C_Q_W),
            _const_spec((1, D_MODEL)), _const_spec((1, D_MODEL)),
            _const_spec((D_MODEL, 3 * D_MODEL)),
            _const_spec((A_Q_W, D_MODEL)), _const_spec((B_GW, D_MODEL)), _const_spec((C_Q_W, D_MODEL)),
            _const_spec((D_MODEL, D_MODEL))],
        out_specs=tok(D_MODEL),
        out_shape=jax.ShapeDtypeStruct((b, s, D_MODEL), F32),
        scratch_shapes=[pltpu.VMEM((MERGE_STAGE_SLOTS, tm // MERGE_SUBTILES, LANES), F32)],
        compiler_params=_params(2),
        name="merge",
    )(*xs, oa, *obs, *lses, oc, g_pre, g_post, w_gate, w_oa, w_ob, w_oc, w_out)


def _ffn_kernel(x_ref, g_pre_ref, g_post_ref, w_in_ref, w_out_ref, out_ref, act_ref, *, n_sub):
    sub_rows = x_ref.shape[0] // n_sub
    for sub in range(n_sub):
        rows = slice(sub * sub_rows, (sub + 1) * sub_rows)
        x = x_ref[rows, :]
        h = _rmsnorm(x, g_pre_ref[...]).astype(BF16)
        for c0 in range(0, FFN_HIDDEN, FFN_CHUNK):
            g = _dot(h, w_in_ref[:, c0:c0 + FFN_CHUNK])
            u = _dot(h, w_in_ref[:, FFN_HIDDEN + c0:FFN_HIDDEN + c0 + FFN_CHUNK])
            act_ref[rows, c0:c0 + FFN_CHUNK] = (g * jax.nn.sigmoid(g) * u).astype(BF16)
        y = _dot(act_ref[rows, :], w_out_ref[...])
        out_ref[rows, :] = x + _rmsnorm(y, g_post_ref[...])


def _ffn_call(x, g_pre, g_post, w_in, w_out, batch_offset, batch):
    s = x.shape[1]
    tm = TOKEN_TILE
    return pl.pallas_call(
        functools.partial(_ffn_kernel, n_sub=FFN_SUBTILES),
        grid=(batch, s // tm),
        in_specs=[pl.BlockSpec((None, tm, D_MODEL), lambda i, j: (i + batch_offset, j, 0)),
                  _const_spec((1, D_MODEL)), _const_spec((1, D_MODEL)),
                  _const_spec((D_MODEL, 2 * FFN_HIDDEN)), _const_spec((FFN_HIDDEN, D_MODEL))],
        out_specs=pl.BlockSpec((None, tm, D_MODEL), lambda i, j: (i, j, 0)),
        out_shape=jax.ShapeDtypeStruct((batch, s, D_MODEL), F32),
        scratch_shapes=[pltpu.VMEM((tm, FFN_HIDDEN), BF16)],
        compiler_params=_params(2),
        name="ffn",
    )(x, g_pre, g_post, w_in, w_out)


def _merge_ffn_kernel(*refs, n_x, split, n_sub):
    n_merge = n_x + 15
    g_pre_ref, g_post_ref, w_in_ref, w_out_ref, out_ref, stage_ref, x1_ref, act_ref = refs[n_merge:]
    _merge_kernel(*refs[:n_merge], x1_ref, stage_ref, n_x=n_x, split=split, n_sub=n_sub)
    _ffn_kernel(x1_ref, g_pre_ref, g_post_ref, w_in_ref, w_out_ref, out_ref, act_ref, n_sub=n_sub)


def _merge_ffn_call(xs, oa, obs, lses, oc, w):
    b, _, s, _ = oa.shape
    tm = FUSED_TILE
    tok = lambda width: pl.BlockSpec((None, tm, width), lambda i, j: (i, j, 0))
    slab = lambda n: pl.BlockSpec((None, n, tm, LANES), lambda i, j: (i, 0, j, 0))
    groups = [_class_spec(dil, tm) for _ in range(2) for _, dil in B_GROUPS]
    return pl.pallas_call(
        functools.partial(_merge_ffn_kernel, n_x=len(xs), split=xs[0].shape[0], n_sub=FUSED_SUBTILES),
        grid=(b, s // tm),
        in_specs=_x_specs(xs, tm) + [slab(A_SLABS)] + groups + [
            tok(C_Q_W),
            _const_spec((1, D_MODEL)), _const_spec((1, D_MODEL)),
            _const_spec((D_MODEL, 3 * D_MODEL)),
            _const_spec((A_Q_W, D_MODEL)), _const_spec((B_GW, D_MODEL)), _const_spec((C_Q_W, D_MODEL)),
            _const_spec((D_MODEL, D_MODEL)),
            _const_spec((1, D_MODEL)), _const_spec((1, D_MODEL)),
            _const_spec((D_MODEL, 2 * FFN_HIDDEN)), _const_spec((FFN_HIDDEN, D_MODEL))],
        out_specs=tok(D_MODEL),
        out_shape=jax.ShapeDtypeStruct((b, s, D_MODEL), F32),
        scratch_shapes=[pltpu.VMEM((MERGE_STAGE_SLOTS, tm // FUSED_SUBTILES, LANES), F32),
                        pltpu.VMEM((tm, D_MODEL), F32), pltpu.VMEM((tm, FFN_HIDDEN), BF16)],
        compiler_params=_params(2),
        name="merge_ffn",
    )(*xs, oa, *obs, *lses, oc, w["g_mix_pre"], w["g_mix_post"], w["w_gate"], w["w_oa"], w["w_ob"], w["w_oc"],
      w["w_out"], w["g_ffn_pre"], w["g_ffn_post"], w["w_ffn_in"], w["w_ffn_out"])


def _rope_tables(seq):
    inv_freq = ROPE_THETA ** (-jnp.arange(0, ROPE_DIMS, 2, dtype=F32) / ROPE_DIMS)
    ang = jnp.arange(seq, dtype=F32)[:, None] * inv_freq[None, :]
    cos, sin = jnp.cos(ang), jnp.sin(ang)
    pad = jnp.zeros((seq, HEAD_DIM - ROPE_DIMS), F32)
    zero = jnp.zeros_like(sin)
    c = jnp.concatenate([cos, cos, pad + 1.0], axis=-1)
    s_up = jnp.concatenate([-sin, zero, pad], axis=-1)
    s_dn = jnp.concatenate([zero, sin, pad], axis=-1)
    tabs = jnp.stack([jnp.tile(t, (1, LANES // HEAD_DIM)) for t in (c, s_up, s_dn)])
    return jnp.concatenate([tabs * (HEAD_DIM ** -0.5 * LOG2E), tabs], axis=0)


def _layer_weights(l, norm_mix_pre, norm_mix_post, w_in, sink_a, w_o_a, w_o_b, w_o_c, w_out,
                   norm_ffn_pre, norm_ffn_post, w_ffn_in, w_ffn_out):
    order = jnp.array(A_HEAD_ORDER)
    w_in_l = w_in[l]
    w_qa = w_in_l[:, :A_Q_W].reshape(D_MODEL, A_HEADS, HEAD_DIM)[:, order].reshape(D_MODEL, A_Q_W)
    w_qkv = jnp.concatenate([w_qa, w_in_l[:, A_Q_W:QKV_W]], axis=1).astype(BF16)
    w_oa = w_o_a[l].reshape(A_HEADS, HEAD_DIM, D_MODEL)[order].reshape(A_Q_W, D_MODEL).astype(BF16)
    row = lambda g: g[l].reshape(1, D_MODEL)
    return dict(
        g_mix_pre=row(norm_mix_pre), g_mix_post=row(norm_mix_post), w_qkv=w_qkv,
        w_gate=w_in_l[:, QKV_W:].astype(BF16), sink=sink_a[l],
        w_oa=w_oa, w_ob=w_o_b[l].astype(BF16), w_oc=w_o_c[l].astype(BF16), w_out=w_out[l].astype(BF16),
        g_ffn_pre=row(norm_ffn_pre), g_ffn_post=row(norm_ffn_post),
        w_ffn_in=w_ffn_in[l].astype(BF16), w_ffn_out=w_ffn_out[l].astype(BF16))


@jax.jit
def kernel(x_prompt, x_sample, mem_prompt, mem_sample, norm_mix_pre, norm_mix_post, norm_mem, w_in, sink_a,
           w_mem_kv, w_o_a, w_o_b, w_o_c, w_out, norm_ffn_pre, norm_ffn_post, w_ffn_in, w_ffn_out):
    layers = [_layer_weights(l, norm_mix_pre, norm_mix_post, w_in, sink_a, w_o_a, w_o_b, w_o_c, w_out,
                             norm_ffn_pre, norm_ffn_post, w_ffn_in, w_ffn_out) for l in range(DEPTH)]
    rope_tabs = _rope_tables(x_prompt.shape[1])
    mkv = _memkv_call(jnp.concatenate([mem_prompt, mem_sample], axis=0),
                      norm_mem.reshape(DEPTH, 1, D_MODEL), w_mem_kv.astype(BF16))
    batches = (x_prompt.shape[0], x_sample.shape[0])
    xs = (x_prompt, x_sample)
    for l, w in enumerate(layers):
        qa, ka, va, qb, kb, vb, oc = _proj_call(xs, w["g_mix_pre"], w["w_qkv"], rope_tabs, mkv, l)
        oa, obs, lses = _attn_call(w["sink"], qa, ka, va, qb, kb, vb)
        if l + 1 < DEPTH:
            xs = (_merge_ffn_call(xs, oa, obs, lses, oc, w),)
    x = _merge_call(xs, oa, obs, lses, oc, w["g_mix_pre"], w["g_mix_post"], w["w_gate"],
                    w["w_oa"], w["w_ob"], w["w_oc"], w["w_out"])
    ffn = functools.partial(_ffn_call, x, w["g_ffn_pre"], w["g_ffn_post"], w["w_ffn_in"], w["w_ffn_out"])
    return (ffn(0, batches[0]), ffn(batches[0], batches[1]))
```

```python
import functools
import math

import jax
import jax.numpy as jnp
from jax import lax
from jax.experimental import pallas as pl
from jax.experimental.pallas import tpu as pltpu

D_MODEL = 1024
DEPTH = 4
HEAD_DIM = 64
A_HEADS = 8
A_KV_HEADS = 2
A_RADIUS = 128
B_GROUPS = ((128, 1), (512, 4), (2048, 16))
B_HPG = 4
C_HEADS = 4
C_HEAD_DIM = 128
N_MEM = 256
ROPE_THETA = 500000.0
ROPE_DIMS = HEAD_DIM // 4
ROPE_HALF = ROPE_DIMS // 2
FFN_HIDDEN = 2816
A_Q_W = A_HEADS * HEAD_DIM
A_KV_W = A_KV_HEADS * HEAD_DIM
B_GW = B_HPG * HEAD_DIM
B_W = B_GW * len(B_GROUPS)
C_Q_W = C_HEADS * C_HEAD_DIM
QKV_W = A_Q_W + 2 * A_KV_W + 3 * B_W + C_Q_W
EPS = 1e-6
NEG_INF = -1e30
LOG2E = math.log2(math.e)

LANES = 128
VMEM_LIMIT = 56 * 1024 * 1024
A_SLABS = A_Q_W // LANES
G_SLABS = B_GW // LANES
CLASS_STRIDE = 4

TOKEN_TILE = 1024
PROJ_SUBTILES = 1
PROJ_STAGE_SLOTS = 6
MERGE_STAGE_SLOTS = 12
FFN_CHUNK = 256
FUSED_TILE = 512
FUSED_SUBTILES = 2
MEMKV_BATCH = 8
ATTN_TQ = 128

BF16 = jnp.bfloat16
F32 = jnp.float32

A_HEAD_ORDER = (0, 4, 1, 5, 2, 6, 3, 7)


def _const_spec(shape):
    nd = len(shape)
    return pl.BlockSpec(shape, lambda *_: (0,) * nd, pipeline_mode=pl.Buffered(1))


def _params(n_grid):
    return pltpu.CompilerParams(dimension_semantics=("arbitrary",) * n_grid,
                                vmem_limit_bytes=VMEM_LIMIT)


def _rmsnorm(x, g):
    return x * lax.rsqrt(jnp.mean(x * x, axis=-1, keepdims=True) + EPS) * g


def _dot(a, b):
    return jnp.dot(a, b, preferred_element_type=F32)


def _dot_nt(a, b):
    return lax.dot_general(a, b, (((1,), (1,)), ((), ())), preferred_element_type=F32)


def _x_specs(xs, tm):
    if len(xs) == 1:
        return [pl.BlockSpec((None, tm, D_MODEL), lambda i, j: (i, j, 0))]
    b0 = xs[0].shape[0]
    last_j = xs[0].shape[1] // tm - 1
    first = pl.BlockSpec((None, tm, D_MODEL),
                         lambda i, j: (jnp.minimum(i, b0 - 1), jnp.where(i < b0, j, last_j), 0))
    second = pl.BlockSpec((None, tm, D_MODEL),
                          lambda i, j: (jnp.maximum(i - b0, 0), jnp.where(i < b0, 0, j), 0))
    return [first, second]


def _class_shape(b, s, dil, dtype):
    return jax.ShapeDtypeStruct((b, G_SLABS, dil, s // dil, LANES), dtype)


def _class_spec(dil, tm):
    return pl.BlockSpec((None, G_SLABS, dil, tm // dil, LANES), lambda i, j: (i, 0, 0, j, 0))


def _read_x(x_refs, split, rows):
    if len(x_refs) == 1:
        return x_refs[0][rows, :]
    return jnp.where(pl.program_id(0) < split, x_refs[0][rows, :], x_refs[1][rows, :])


def _memkv_kernel(mem_ref, g_ref, w_ref, o_ref):
    nb = mem_ref.shape[0]
    h = _rmsnorm(mem_ref[...].reshape(nb * N_MEM, D_MODEL), g_ref[...]).astype(BF16)
    o_ref[...] = _dot(h, w_ref[...]).astype(BF16).reshape(nb, N_MEM, 2 * C_Q_W)


def _memkv_call(mem, g_mem, w_mem_kv):
    b = mem.shape[0]
    nb = math.gcd(b, MEMKV_BATCH)
    return pl.pallas_call(
        _memkv_kernel,
        grid=(DEPTH, b // nb),
        in_specs=[
            pl.BlockSpec((nb, N_MEM, D_MODEL), lambda l, i: (i, 0, 0)),
            pl.BlockSpec((None, 1, D_MODEL), lambda l, i: (l, 0, 0)),
            pl.BlockSpec((None, D_MODEL, 2 * C_Q_W), lambda l, i: (l, 0, 0)),
        ],
        out_specs=pl.BlockSpec((None, nb, N_MEM, 2 * C_Q_W), lambda l, i: (l, i, 0, 0)),
        out_shape=jax.ShapeDtypeStruct((DEPTH, b, N_MEM, 2 * C_Q_W), BF16),
        compiler_params=_params(2),
        name="memkv",
    )(mem, g_mem, w_mem_kv)


def _rope_tile(y, c, s_up, s_dn):
    return (y * c + pltpu.roll(y, LANES - ROPE_HALF, 1) * s_up + pltpu.roll(y, ROPE_HALF, 1) * s_dn)


def _proj_kernel(*refs, n_x, split, n_sub):
    x_refs = refs[:n_x]
    (g_ref, w_ref, rope_ref, mkv_ref, qa_ref, ka_ref, va_ref) = refs[n_x:n_x + 7]
    b_refs = refs[n_x + 7:n_x + 16]
    oc_ref, stage_ref = refs[n_x + 16:]
    sub_rows = x_refs[0].shape[0] // n_sub
    n_stage = 0
    for sub in range(n_sub):
        rows = slice(sub * sub_rows, (sub + 1) * sub_rows)
        h = _rmsnorm(_read_x(x_refs, split, rows), g_ref[...]).astype(BF16)

        def project(col0, width, store, rope_base):
            y = _dot(h, w_ref[:, col0:col0 + width])
            for t in range(width // LANES):
                y_t = y[:, t * LANES:(t + 1) * LANES]
                if rope_base is not None:
                    y_t = _rope_tile(y_t, rope_ref[rope_base, rows, :], rope_ref[rope_base + 1, rows, :],
                                     rope_ref[rope_base + 2, rows, :])
                store(t, y_t)

        def slab_store(ref):
            def store(t, val):
                ref[t, rows, :] = val.astype(BF16)
            return store

        def flat_store(ref):
            def store(t, val):
                ref[rows, t * LANES:(t + 1) * LANES] = val.astype(BF16)
            return store

        def class_store(group_refs):
            def store(t, val):
                nonlocal n_stage
                (_, dil), ref = B_GROUPS[t // G_SLABS], group_refs[t // G_SLABS]
                n = sub_rows // dil
                dst = slice(sub * n, (sub + 1) * n)
                if dil == 1:
                    ref[t % G_SLABS, 0, dst, :] = val.astype(BF16)
                    return
                slot = stage_ref.at[n_stage % stage_ref.shape[0]]
                slot[...] = val
                if dil == CLASS_STRIDE:
                    n_stage += 1
                    for c in range(dil):
                        ref[t % G_SLABS, c, dst, :] = slot[pl.ds(c, n, stride=CLASS_STRIDE), :].astype(BF16)
                    return
                assert dil == CLASS_STRIDE * CLASS_STRIDE
                quarter = sub_rows // CLASS_STRIDE
                slot2 = stage_ref.at[(n_stage + 1) % stage_ref.shape[0]]
                n_stage += 2
                for c0 in range(CLASS_STRIDE):
                    slot2[c0 * quarter:(c0 + 1) * quarter, :] = slot[pl.ds(c0, quarter, stride=CLASS_STRIDE), :]
                for c0 in range(CLASS_STRIDE):
                    for c1 in range(CLASS_STRIDE):
                        ref[t % G_SLABS, CLASS_STRIDE * c1 + c0, dst, :] = (
                            slot2[pl.ds(c0 * quarter + c1, n, stride=CLASS_STRIDE), :].astype(BF16))
            return store

        rope_q, rope_k = 0, 3
        col = 0
        project(col, A_Q_W, slab_store(qa_ref), rope_q); col += A_Q_W
        project(col, A_KV_W, flat_store(ka_ref), rope_k); col += A_KV_W
        project(col, A_KV_W, flat_store(va_ref), None); col += A_KV_W
        project(col, B_W, class_store(b_refs[0:3]), rope_q); col += B_W
        project(col, B_W, class_store(b_refs[3:6]), rope_k); col += B_W
        project(col, B_W, class_store(b_refs[6:9]), None); col += B_W

        qc = (_dot(h, w_ref[:, col:col + C_Q_W]) * (C_HEAD_DIM ** -0.5 * LOG2E)).astype(BF16)
        for hd in range(C_HEADS):
            lo, hi = hd * C_HEAD_DIM, (hd + 1) * C_HEAD_DIM
            s = _dot_nt(qc[:, lo:hi], mkv_ref[:, lo:hi])
            m = jnp.max(s, axis=-1, keepdims=True)
            p = jnp.exp2(s - m)
            denom = jnp.sum(p, axis=-1, keepdims=True)
            o = _dot(p.astype(BF16), mkv_ref[:, C_Q_W + lo:C_Q_W + hi])
            oc_ref[rows, lo:hi] = (o * (1.0 / denom)).astype(BF16)


def _proj_call(xs, g_pre, w_qkv, rope_tabs, mkv, layer):
    b = sum(x.shape[0] for x in xs)
    s = xs[0].shape[1]
    tm = TOKEN_TILE
    tok = lambda w: pl.BlockSpec((None, tm, w), lambda i, j: (i, j, 0))
    slab = lambda n: pl.BlockSpec((None, n, tm, LANES), lambda i, j: (i, 0, j, 0))
    slab_shape = lambda n: jax.ShapeDtypeStruct((b, n, s, LANES), BF16)
    flat_shape = lambda w: jax.ShapeDtypeStruct((b, s, w), BF16)
    group_specs = [_class_spec(dil, tm) for _ in range(3) for _, dil in B_GROUPS]
    group_shapes = [_class_shape(b, s, dil, BF16) for _ in range(3) for _, dil in B_GROUPS]
    outs = pl.pallas_call(
        functools.partial(_proj_kernel, n_x=len(xs), split=xs[0].shape[0], n_sub=PROJ_SUBTILES),
        grid=(b, s // tm),
        in_specs=_x_specs(xs, tm) + [
            _const_spec((1, D_MODEL)),
            _const_spec((D_MODEL, QKV_W)),
            pl.BlockSpec((6, tm, LANES), lambda i, j: (0, j, 0)),
            pl.BlockSpec((None, None, N_MEM, 2 * C_Q_W), lambda i, j: (layer, i, 0, 0)),
        ],
        out_specs=[slab(A_SLABS), tok(A_KV_W), tok(A_KV_W)] + group_specs + [tok(C_Q_W)],
        out_shape=[slab_shape(A_SLABS), flat_shape(A_KV_W), flat_shape(A_KV_W)] + group_shapes
                  + [flat_shape(C_Q_W)],
        scratch_shapes=[pltpu.VMEM((PROJ_STAGE_SLOTS, tm // PROJ_SUBTILES, LANES), F32)],
        compiler_params=_params(2),
        name="proj",
    )(*xs, g_pre, w_qkv, rope_tabs, mkv)
    qa, ka, va = outs[:3]
    return qa, ka, va, outs[3:6], outs[6:9], outs[9:12], outs[12]


def _head_lane_masks():
    lane = lax.broadcasted_iota(jnp.int32, (1, LANES), 1)
    lo = (lane < HEAD_DIM).astype(BF16)
    return lo, (1 - lo).astype(BF16)


def _band_bias(tq, kw, q0_minus_k0, radius):
    d = (lax.broadcasted_iota(jnp.int32, (tq, kw), 0) + q0_minus_k0
         - lax.broadcasted_iota(jnp.int32, (tq, kw), 1))
    return jnp.where(jnp.abs(d) <= radius, 0.0, NEG_INF).astype(F32)


def _attend(q_t, k_pair, v_pair, bias, lane_is_lo, sinks):
    kw = k_pair.shape[0] // 2
    s = _dot_nt(q_t, k_pair)
    ms, ps = [], []
    for half in range(2):
        s_h = (s[:, half * kw:(half + 1) * kw] + bias).astype(BF16)
        m_h = jnp.broadcast_to(jnp.max(s_h, axis=-1, keepdims=True), (s_h.shape[0], LANES))
        if sinks is not None:
            m_h = jnp.maximum(m_h, sinks[half].astype(BF16))
        ps.append(jnp.exp2(s_h - jnp.concatenate([m_h] * (kw // LANES), axis=1)))
        ms.append(m_h.astype(F32))
    m = jnp.where(lane_is_lo, ms[0], ms[1])
    r = _dot(jnp.concatenate(ps, axis=1), v_pair)
    denom = r[:, LANES:]
    if sinks is not None:
        denom = denom + jnp.exp2(jnp.where(lane_is_lo, sinks[0], sinks[1]) - m)
    return r[:, :LANES] * (1.0 / denom), m, denom


def _fill_bias_table(bias_ref, tq, kw, radius):
    for n in range(bias_ref.shape[0]):
        bias_ref[n] = _band_bias(tq, kw, n * radius, radius)


def _pair_kv(k, v, lo, hi):
    k_pair = jnp.concatenate([k * lo, k * hi], axis=0)
    v_pair = jnp.concatenate([jnp.concatenate([v * lo, jnp.broadcast_to(lo, v.shape)], axis=1),
                              jnp.concatenate([v * hi, jnp.broadcast_to(hi, v.shape)], axis=1)], axis=0)
    return k_pair, v_pair


def _attend_rows(q_t, k, v, bias, lo, hi, lane_is_lo):
    tq, kw = q_t.shape[0], k.shape[0]
    s = _dot_nt(jnp.concatenate([q_t * lo, q_t * hi], axis=0), k)
    s = (s.reshape(2, tq, kw) + bias[None]).reshape(2 * tq, kw).astype(BF16)
    m = jnp.broadcast_to(jnp.max(s, axis=-1, keepdims=True), (2 * tq, LANES))
    p = jnp.exp2(s - jnp.concatenate([m] * (kw // LANES), axis=1))
    r = _dot(p, jnp.concatenate([v, jnp.ones(v.shape, BF16)], axis=1))
    o = jnp.where(lane_is_lo, r[:tq, :LANES], r[tq:, :LANES])
    denom = jnp.where(lane_is_lo, r[:tq, LANES:], r[tq:, LANES:])
    m = m.astype(F32)
    return o * (1.0 / denom), jnp.where(lane_is_lo, m[:tq], m[tq:]), denom


def _attn_a_kernel(sink_ref, q_ref, k_ref, v_ref, o_ref, bias_ref, *, tq, kw, seq):
    lo, hi = _head_lane_masks()
    lane_is_lo = lax.broadcasted_iota(jnp.int32, (1, LANES), 1) < HEAD_DIM
    sinks = [jnp.full((tq, LANES), sink_ref[h] * LOG2E, F32) for h in range(A_HEADS)]
    _fill_bias_table(bias_ref, tq, kw, A_RADIUS)

    for i in range(seq // tq):
        q0 = i * tq
        start = min(max(q0 - A_RADIUS, 0), seq - kw)
        bias = bias_ref[(q0 - start) // A_RADIUS]
        k_pair, v_pair = _pair_kv(k_ref[pl.ds(start, kw), :], v_ref[pl.ds(start, kw), :], lo, hi)
        for t in range(A_SLABS):
            q_t = q_ref[t, pl.ds(q0, tq), :]
            o, _, _ = _attend(q_t, k_pair, v_pair, bias, lane_is_lo,
                              (sinks[A_HEAD_ORDER[2 * t]], sinks[A_HEAD_ORDER[2 * t + 1]]))
            o_ref[t, pl.ds(q0, tq), :] = o.astype(BF16)


def _attn_b_kernel(q_ref, k_ref, v_ref, o_ref, lse_ref, bias_ref, *, dil, length, radius, tq, kw):
    lo, hi = _head_lane_masks()
    lane_is_lo = lax.broadcasted_iota(jnp.int32, (1, LANES), 1) < HEAD_DIM
    nblk = length // tq
    _fill_bias_table(bias_ref, tq, kw, radius)

    for u in range(dil * nblk):
        c, i = divmod(u, nblk)
        q0 = i * tq
        start = min(max(q0 - radius, 0), length - kw)
        bias = bias_ref[(q0 - start) // radius]
        q_rows, win = pl.ds(q0, tq), pl.ds(start, kw)
        for t in range(G_SLABS):
            q_t, k, v = q_ref[t, c, q_rows, :], k_ref[t, c, win, :], v_ref[t, c, win, :]
            if kw >= 2 * LANES:
                o, m, denom = _attend_rows(q_t, k, v, bias, lo, hi, lane_is_lo)
            else:
                o, m, denom = _attend(q_t, *_pair_kv(k, v, lo, hi), bias, lane_is_lo, None)
            o_ref[t, c, q_rows, :] = o.astype(BF16)
            lse_ref[t, c, q_rows, :] = m + jnp.log2(denom)


def _b_geometry(seq, group):
    window, dil = B_GROUPS[group]
    length = seq // dil
    radius = window // (2 * dil)
    tq = min(ATTN_TQ, length)
    return dict(dil=dil, length=length, radius=radius, tq=tq, kw=min(tq + 2 * radius, length))


def _attn_kernel(sink_ref, qa_ref, ka_ref, va_ref, *refs, seq):
    n = len(B_GROUPS)
    qkv_refs, refs = refs[:3 * n], refs[3 * n:]
    oa_ref, out_refs, bias_refs = refs[0], refs[1:1 + 2 * n], refs[1 + 2 * n:]
    _attn_a_kernel(sink_ref, qa_ref, ka_ref, va_ref, oa_ref, bias_refs[0],
                   tq=ATTN_TQ, kw=ATTN_TQ + 2 * A_RADIUS, seq=seq)
    for g in range(n):
        _attn_b_kernel(*qkv_refs[3 * g:3 * g + 3], *out_refs[2 * g:2 * g + 2], bias_refs[1 + g],
                       **_b_geometry(seq, g))


def _attn_call(sink, qa, ka, va, qb, kb, vb):
    b, _, s, _ = qa.shape
    whole = lambda shape: pl.BlockSpec((None,) + tuple(shape[1:]), lambda i: (i,) + (0,) * (len(shape) - 1))
    geo = [_b_geometry(s, g) for g in range(len(B_GROUPS))]
    qkv = [a for g in range(len(B_GROUPS)) for a in (qb[g], kb[g], vb[g])]
    out_shape = [jax.ShapeDtypeStruct(qa.shape, BF16)]
    for g in geo:
        out_shape += [_class_shape(b, s, g["dil"], BF16), _class_shape(b, s, g["dil"], F32)]
    bias = [pltpu.VMEM((3, ATTN_TQ, ATTN_TQ + 2 * A_RADIUS), F32)]
    bias += [pltpu.VMEM((3 if g["length"] > g["tq"] else 1, g["tq"], g["kw"]), F32) for g in geo]
    outs = pl.pallas_call(
        functools.partial(_attn_kernel, seq=s),
        grid=(b,),
        in_specs=[pl.BlockSpec(memory_space=pltpu.SMEM)] + [whole(a.shape) for a in (qa, ka, va, *qkv)],
        out_specs=[whole(o.shape) for o in out_shape],
        out_shape=out_shape,
        scratch_shapes=bias,
        compiler_params=_params(1),
        name="attn",
    )(sink, qa, ka, va, *qkv)
    return outs[0], outs[1::2], outs[2::2]


def _lanes(ref, rows):
    return jnp.concatenate([ref[t, rows, :] for t in range(ref.shape[0])], axis=-1)


def _merge_kernel(*refs, n_x, split, n_sub):
    x_refs = refs[:n_x]
    (oa_ref, ob0_ref, ob1_ref, ob2_ref, l0_ref, l1_ref, l2_ref, oc_ref,
     g_pre_ref, g_post_ref, w_gate_ref, w_oa_ref, w_ob_ref, w_oc_ref, w_out_ref, out_ref, stage_ref) = refs[n_x:]
    sub_rows = out_ref.shape[0] // n_sub
    n_stage = 0

    def natural(ref, sub):
        nonlocal n_stage
        dil = ref.shape[1]
        n = sub_rows // dil
        src = slice(sub * n, (sub + 1) * n)
        tiles = []
        for t in range(G_SLABS):
            if dil == 1:
                tiles.append(ref[t, 0, src, :].astype(F32))
                continue
            slot = stage_ref.at[n_stage % stage_ref.shape[0]]
            if dil == CLASS_STRIDE:
                n_stage += 1
                for c in range(dil):
                    slot[pl.ds(c, n, stride=CLASS_STRIDE), :] = ref[t, c, src, :].astype(F32)
                tiles.append(slot[...])
                continue
            assert dil == CLASS_STRIDE * CLASS_STRIDE
            quarter = sub_rows // CLASS_STRIDE
            slot2 = stage_ref.at[(n_stage + 1) % stage_ref.shape[0]]
            n_stage += 2
            for c0 in range(CLASS_STRIDE):
                for c1 in range(CLASS_STRIDE):
                    slot[pl.ds(c0 * quarter + c1, n, stride=CLASS_STRIDE), :] = (
                        ref[t, CLASS_STRIDE * c1 + c0, src, :].astype(F32))
            for c0 in range(CLASS_STRIDE):
                slot2[pl.ds(c0, quarter, stride=CLASS_STRIDE), :] = slot[c0 * quarter:(c0 + 1) * quarter, :]
            tiles.append(slot2[...])
        return jnp.concatenate(tiles, axis=-1)

    for sub in range(n_sub):
        rows = slice(sub * sub_rows, (sub + 1) * sub_rows)
        x = _read_x(x_refs, split, rows)
        h = _rmsnorm(x, g_pre_ref[...]).astype(BF16)

        l0, l1, l2 = natural(l0_ref, sub), natural(l1_ref, sub), natural(l2_ref, sub)
        m = jnp.maximum(jnp.maximum(l0, l1), l2)
        e0, e1, e2 = jnp.exp2(l0 - m), jnp.exp2(l1 - m), jnp.exp2(l2 - m)
        num = e0 * natural(ob0_ref, sub) + e1 * natural(ob1_ref, sub) + e2 * natural(ob2_ref, sub)
        ob = (num * (1.0 / (e0 + e1 + e2))).astype(BF16)

        merged = None
        for br, (o_br, w_ref) in enumerate(((_lanes(oa_ref, rows), w_oa_ref), (ob, w_ob_ref),
                                            (oc_ref[rows, :], w_oc_ref))):
            gate = jax.nn.sigmoid(_dot(h, w_gate_ref[:, br * D_MODEL:(br + 1) * D_MODEL]))
            term = gate * _dot(o_br, w_ref[...])
            merged = term if merged is None else merged + term
        z = _dot(merged.astype(BF16), w_out_ref[...])
        out_ref[rows, :] = x + _rmsnorm(z, g_post_ref[...])


def _ffn_kernel(x_ref, g_pre_ref, g_post_ref, w_in_ref, w_out_ref, out_ref, act_ref, *, n_sub):
    sub_rows = x_ref.shape[0] // n_sub
    for sub in range(n_sub):
        rows = slice(sub * sub_rows, (sub + 1) * sub_rows)
        x = x_ref[rows, :]
        h = _rmsnorm(x, g_pre_ref[...]).astype(BF16)
        for c0 in range(0, FFN_HIDDEN, FFN_CHUNK):
            g = _dot(h, w_in_ref[:, c0:c0 + FFN_CHUNK])
            u = _dot(h, w_in_ref[:, FFN_HIDDEN + c0:FFN_HIDDEN + c0 + FFN_CHUNK])
            act_ref[rows, c0:c0 + FFN_CHUNK] = (g * jax.nn.sigmoid(g) * u).astype(BF16)
        y = _dot(act_ref[rows, :], w_out_ref[...])
        out_ref[rows, :] = x + _rmsnorm(y, g_post_ref[...])


def _merge_ffn_kernel(*refs, n_x, split, n_sub):
    n_merge = n_x + 15
    g_pre_ref, g_post_ref, w_in_ref, w_out_ref, out_ref, stage_ref, x1_ref, act_ref = refs[n_merge:]
    _merge_kernel(*refs[:n_merge], x1_ref, stage_ref, n_x=n_x, split=split, n_sub=n_sub)
    _ffn_kernel(x1_ref, g_pre_ref, g_post_ref, w_in_ref, w_out_ref, out_ref, act_ref, n_sub=n_sub)


def _merge_ffn_call(xs, oa, obs, lses, oc, w, batch_offset=0, batch=None):
    s = oa.shape[2]
    b = oa.shape[0] if batch is None else batch
    off = batch_offset
    assert off == 0 or len(xs) == 1
    tm = FUSED_TILE
    tok = lambda width: pl.BlockSpec((None, tm, width), lambda i, j: (i + off, j, 0))
    slab = lambda n: pl.BlockSpec((None, n, tm, LANES), lambda i, j: (i + off, 0, j, 0))
    groups = [pl.BlockSpec((None, G_SLABS, dil, tm // dil, LANES), lambda i, j: (i + off, 0, 0, j, 0))
              for _ in range(2) for _, dil in B_GROUPS]
    x_specs = [tok(D_MODEL)] if len(xs) == 1 else _x_specs(xs, tm)
    return pl.pallas_call(
        functools.partial(_merge_ffn_kernel, n_x=len(xs), split=xs[0].shape[0], n_sub=FUSED_SUBTILES),
        grid=(b, s // tm),
        in_specs=x_specs + [slab(A_SLABS)] + groups + [
            tok(C_Q_W),
            _const_spec((1, D_MODEL)), _const_spec((1, D_MODEL)),
            _const_spec((D_MODEL, 3 * D_MODEL)),
            _const_spec((A_Q_W, D_MODEL)), _const_spec((B_GW, D_MODEL)), _const_spec((C_Q_W, D_MODEL)),
            _const_spec((D_MODEL, D_MODEL)),
            _const_spec((1, D_MODEL)), _const_spec((1, D_MODEL)),
            _const_spec((D_MODEL, 2 * FFN_HIDDEN)), _const_spec((FFN_HIDDEN, D_MODEL))],
        out_specs=pl.BlockSpec((None, tm, D_MODEL), lambda i, j: (i, j, 0)),
        out_shape=jax.ShapeDtypeStruct((b, s, D_MODEL), F32),
        scratch_shapes=[pltpu.VMEM((MERGE_STAGE_SLOTS, tm // FUSED_SUBTILES, LANES), F32),
                        pltpu.VMEM((tm, D_MODEL), F32), pltpu.VMEM((tm, FFN_HIDDEN), BF16)],
        compiler_params=_params(2),
        name="merge_ffn",
    )(*xs, oa, *obs, *lses, oc, w["g_mix_pre"], w["g_mix_post"], w["w_gate"], w["w_oa"], w["w_ob"], w["w_oc"],
      w["w_out"], w["g_ffn_pre"], w["g_ffn_post"], w["w_ffn_in"], w["w_ffn_out"])


def _rope_tables(seq):
    inv_freq = ROPE_THETA ** (-jnp.arange(0, ROPE_DIMS, 2, dtype=F32) / ROPE_DIMS)
    ang = jnp.arange(seq, dtype=F32)[:, None] * inv_freq[None, :]
    cos, sin = jnp.cos(ang), jnp.sin(ang)
    pad = jnp.zeros((seq, HEAD_DIM - ROPE_DIMS), F32)
    zero = jnp.zeros_like(sin)
    c = jnp.concatenate([cos, cos, pad + 1.0], axis=-1)
    s_up = jnp.concatenate([-sin, zero, pad], axis=-1)
    s_dn = jnp.concatenate([zero, sin, pad], axis=-1)
    tabs = jnp.stack([jnp.tile(t, (1, LANES // HEAD_DIM)) for t in (c, s_up, s_dn)])
    return jnp.concatenate([tabs * (HEAD_DIM ** -0.5 * LOG2E), tabs], axis=0)


def _layer_weights(l, norm_mix_pre, norm_mix_post, w_in, sink_a, w_o_a, w_o_b, w_o_c, w_out,
                   norm_ffn_pre, norm_ffn_post, w_ffn_in, w_ffn_out):
    order = jnp.array(A_HEAD_ORDER)
    w_in_l = w_in[l]
    w_qa = w_in_l[:, :A_Q_W].reshape(D_MODEL, A_HEADS, HEAD_DIM)[:, order].reshape(D_MODEL, A_Q_W)
    w_qkv = jnp.concatenate([w_qa, w_in_l[:, A_Q_W:QKV_W]], axis=1).astype(BF16)
    w_oa = w_o_a[l].reshape(A_HEADS, HEAD_DIM, D_MODEL)[order].reshape(A_Q_W, D_MODEL).astype(BF16)
    row = lambda g: g[l].reshape(1, D_MODEL)
    return dict(
        g_mix_pre=row(norm_mix_pre), g_mix_post=row(norm_mix_post), w_qkv=w_qkv,
        w_gate=w_in_l[:, QKV_W:].astype(BF16), sink=sink_a[l],
        w_oa=w_oa, w_ob=w_o_b[l].astype(BF16), w_oc=w_o_c[l].astype(BF16), w_out=w_out[l].astype(BF16),
        g_ffn_pre=row(norm_ffn_pre), g_ffn_post=row(norm_ffn_post),
        w_ffn_in=w_ffn_in[l].astype(BF16), w_ffn_out=w_ffn_out[l].astype(BF16))


@jax.jit
def kernel(x_prompt, x_sample, mem_prompt, mem_sample, norm_mix_pre, norm_mix_post, norm_mem, w_in, sink_a,
           w_mem_kv, w_o_a, w_o_b, w_o_c, w_out, norm_ffn_pre, norm_ffn_post, w_ffn_in, w_ffn_out):
    layers = [_layer_weights(l, norm_mix_pre, norm_mix_post, w_in, sink_a, w_o_a, w_o_b, w_o_c, w_out,
                             norm_ffn_pre, norm_ffn_post, w_ffn_in, w_ffn_out) for l in range(DEPTH)]
    rope_tabs = _rope_tables(x_prompt.shape[1])
    mkv = _memkv_call(jnp.concatenate([mem_prompt, mem_sample], axis=0),
                      norm_mem.reshape(DEPTH, 1, D_MODEL), w_mem_kv.astype(BF16))
    batches = (x_prompt.shape[0], x_sample.shape[0])
    xs = (x_prompt, x_sample)
    for l, w in enumerate(layers):
        qa, ka, va, qb, kb, vb, oc = _proj_call(xs, w["g_mix_pre"], w["w_qkv"], rope_tabs, mkv, l)
        oa, obs, lses = _attn_call(w["sink"], qa, ka, va, qb, kb, vb)
        if l + 1 < DEPTH:
            xs = (_merge_ffn_call(xs, oa, obs, lses, oc, w),)
    last = functools.partial(_merge_ffn_call, xs, oa, obs, lses, oc, w)
    return (last(0, batches[0]), last(batches[0], batches[1]))
```
